```python
import jax, jax.numpy as jnp
from jax import lax
import numpy as np

D_MODEL = 2048
BATCH = 8
SEQ = 4096
DEPTH = 2

PLE_DIM = 256
MIX_WIDTH = D_MODEL // 2
CHUNK = 64
EPS = 1e-6
GLA_HEADS = 4
GLA_DK = MIX_WIDTH // 8
GLA_DV = MIX_WIDTH // GLA_HEADS
GLA_LORA = 16
GLA_GATE_NORMALIZER = 16.0
RWKV_HEAD = 64
RWKV_HEADS = MIX_WIDTH // RWKV_HEAD
RWKV_W_LORA = 64
RWKV_A_LORA = 64
RWKV_V_LORA = 32
RWKV_G_LORA = 160
RWKV_GN_EPS = 64e-5
RET_HEADS = 4
RET_DK = MIX_WIDTH // 8
RET_DV = MIX_WIDTH // RET_HEADS
ROPE_BASE = 10000.0
D_FF = 5504
CONV_W = 3

GLA_QK = GLA_HEADS * GLA_DK
GLA_SPLITS = (GLA_QK, 2 * GLA_QK, 2 * GLA_QK + MIX_WIDTH, 2 * GLA_QK + 2 * MIX_WIDTH)
GLA_COLS = 2 * GLA_QK + 2 * MIX_WIDTH + GLA_LORA
RWKV_SPLITS = (MIX_WIDTH, 2 * MIX_WIDTH, 3 * MIX_WIDTH, 3 * MIX_WIDTH + RWKV_W_LORA,
               3 * MIX_WIDTH + RWKV_W_LORA + RWKV_A_LORA)
RWKV_COLS = 3 * MIX_WIDTH + RWKV_W_LORA + RWKV_A_LORA + RWKV_G_LORA
RET_QK = RET_HEADS * RET_DK
RET_SPLITS = (RET_QK, 2 * RET_QK, 2 * RET_QK + MIX_WIDTH)
RET_COLS = 2 * RET_QK + 2 * MIX_WIDTH
GATE_COLS = 3 * D_MODEL
IN_SPLITS = (GLA_COLS, GLA_COLS + RWKV_COLS, GLA_COLS + RWKV_COLS + RET_COLS)
N_IN = GLA_COLS + RWKV_COLS + RET_COLS + GATE_COLS

kernel_name = "hybrid_gla_rwkv7_retnet_gated_block"


def rmsnorm(x, gain, eps=EPS):
    xf = x.astype(jnp.float32)
    y = xf * lax.rsqrt(jnp.mean(xf * xf, axis=-1, keepdims=True) + eps)
    return (y * gain.astype(jnp.float32)).astype(x.dtype)


def head_rmsnorm(x, eps=EPS):
    return x * lax.rsqrt(jnp.mean(x * x, axis=-1, keepdims=True) + eps)


def to_chunks(t):
    b, t_len, h, d = t.shape
    return t.reshape(b, t_len // CHUNK, CHUNK, h, d).transpose(1, 0, 3, 2, 4)


def from_chunks(t):
    n, b, h, c, d = t.shape
    return t.transpose(1, 0, 3, 2, 4).reshape(b, n * c, h, d)


def token_shift(u, mu):
    prev = jnp.pad(u, ((0, 0), (1, 0), (0, 0)))[:, :-1]
    return u + (prev - u) * mu


def rotary(x, positions):
    d = x.shape[-1]
    inv_freq = ROPE_BASE ** (-jnp.arange(0, d, 2, dtype=jnp.float32) / d)
    ang = positions.astype(jnp.float32)[:, :, None, None] * inv_freq
    cos, sin = jnp.cos(ang), jnp.sin(ang)
    x1, x2 = jnp.split(x, 2, axis=-1)
    return jnp.concatenate([x1 * cos - x2 * sin, x1 * sin + x2 * cos], axis=-1)


def gla_chunked(q, k, v, log_a):
    b_, _, h_, dk = q.shape
    dv = v.shape[-1]
    qc, kc, vc, gc = map(to_chunks, (q, k, v, log_a))
    cum = jnp.cumsum(gc, axis=3)
    cum_last = cum[:, :, :, -1:, :]
    q_in = qc * jnp.exp(cum)
    k_in = kc * jnp.exp(-cum)
    k_st = kc * jnp.exp(cum_last - cum)
    causal = jnp.tril(jnp.ones((CHUNK, CHUNK), dtype=bool))
    scores = jnp.where(causal, jnp.einsum('nbhid,nbhjd->nbhij', q_in, k_in), 0.0)
    o_intra = jnp.einsum('nbhij,nbhjv->nbhiv', scores, vc)

    def step(state, xs):
        q_c, k_c, v_c, dec_c = xs
        o = jnp.einsum('bhid,bhdv->bhiv', q_c, state)
        state = state * dec_c[:, :, 0, :, None] + jnp.einsum('bhjd,bhjv->bhdv', k_c, v_c)
        return state, o

    s0 = jnp.zeros((b_, h_, dk, dv), jnp.float32)
    _, o_inter = lax.scan(step, s0, (q_in, k_st, vc, jnp.exp(cum_last)))
    return from_chunks(o_intra + o_inter)


def rwkv7_scan(r, w, k, v, a, b):
    b_, _, h_, n_ = r.shape

    def step(state, xs):
        r_t, w_t, k_t, v_t, a_t, b_t = xs
        sa = jnp.einsum('bhvk,bhk->bhv', state, a_t)
        state = (state * w_t[:, :, None, :] + sa[..., None] * b_t[:, :, None, :]
                 + v_t[..., None] * k_t[:, :, None, :])
        return state, jnp.einsum('bhvk,bhk->bhv', state, r_t)

    s0 = jnp.zeros((b_, h_, n_, n_), jnp.float32)
    xs = tuple(jnp.moveaxis(t, 1, 0) for t in (r, w, k, v, a, b))
    _, y = lax.scan(step, s0, xs)
    return jnp.moveaxis(y, 0, 1)


def retention_chunked(q, k, v, log_gamma):
    b_, _, h_, dk = q.shape
    dv = v.shape[-1]
    qc, kc, vc = map(to_chunks, (q, k, v))
    pos = jnp.arange(CHUNK, dtype=jnp.float32)
    lg = log_gamma[:, None, None]
    diff = pos[:, None] - pos[None, :]
    causal = diff >= 0
    dmask = jnp.where(causal, jnp.exp(lg * jnp.where(causal, diff, 0.0)), 0.0)
    scores = jnp.einsum('nbhid,nbhjd->nbhij', qc, kc) * dmask
    o_intra = jnp.einsum('nbhij,nbhjv->nbhiv', scores, vc)
    q_in = qc * jnp.exp(lg * (pos[:, None] + 1.0))
    v_st = vc * jnp.exp(lg * (CHUNK - 1.0 - pos[:, None]))
    chunk_decay = jnp.exp(lg * CHUNK)

    def step(state, xs):
        q_c, k_c, v_c = xs
        o = jnp.einsum('bhid,bhdv->bhiv', q_c, state)
        state = state * chunk_decay + jnp.einsum('bhjd,bhjv->bhdv', k_c, v_c)
        return state, o

    s0 = jnp.zeros((b_, h_, dk, dv), jnp.float32)
    _, o_inter = lax.scan(step, s0, (q_in, kc, v_st))
    return from_chunks(o_intra + o_inter)


def gla_branch(u, w_decay, b_decay, norm_g):
    b_, t_, _ = u.shape
    q, k, v, g, lr = jnp.split(u.astype(jnp.float32), GLA_SPLITS, axis=-1)
    q = q.reshape(b_, t_, GLA_HEADS, GLA_DK) * GLA_DK ** -0.5
    k = k.reshape(b_, t_, GLA_HEADS, GLA_DK)
    v = v.reshape(b_, t_, GLA_HEADS, GLA_DV)
    log_a = jax.nn.log_sigmoid(lr @ w_decay + b_decay) / GLA_GATE_NORMALIZER
    o = gla_chunked(q, k, v, log_a.reshape(b_, t_, GLA_HEADS, GLA_DK))
    o = head_rmsnorm(o) * norm_g
    return o.reshape(b_, t_, MIX_WIDTH) * jax.nn.silu(g)


def rwkv7_branch(u, mu, w0, w2, a0, a2, g2, k_k, k_a, r_k, ln_w, ln_b, v_first, v_mix):
    b_, t_, _ = u.shape
    heads = lambda t: t.reshape(b_, t_, RWKV_HEADS, RWKV_HEAD)
    us = token_shift(u.astype(jnp.float32), mu)
    r, k, v, wl, al, gl = jnp.split(us, RWKV_SPLITS, axis=-1)
    w_log = -jax.nn.softplus(-(w0 + jnp.tanh(wl) @ w2)) - 0.5
    decay = jnp.exp(-jnp.exp(w_log))
    a = jax.nn.sigmoid(a0 + al @ a2)
    g = jax.nn.sigmoid(gl) @ g2
    kk = heads(k * k_k)
    kk = kk / jnp.maximum(jnp.linalg.norm(kk, axis=-1, keepdims=True), 1e-12)
    k = k * (1.0 + (a - 1.0) * k_a)
    if v_mix is None:
        v_first = v
    else:
        v0, v1, v2 = v_mix
        v = v + (v_first - v) * jax.nn.sigmoid(v0 + (v @ v1) @ v2)
    r_h, k_h, v_h, a_h, w_h = map(heads, (r, k, v, a, decay))
    y = rwkv7_scan(r_h, w_h, k_h, v_h, -kk, kk * a_h)
    mean = jnp.mean(y, axis=-1, keepdims=True)
    var = jnp.mean(jnp.square(y - mean), axis=-1, keepdims=True)
    y = ((y - mean) * lax.rsqrt(var + RWKV_GN_EPS)).reshape(b_, t_, MIX_WIDTH) * ln_w + ln_b
    bonus = jnp.sum(r_h * k_h * r_k, axis=-1, keepdims=True) * v_h
    return (y + bonus.reshape(b_, t_, MIX_WIDTH)) * g, v_first


def retention_branch(u, positions):
    b_, t_, _ = u.shape
    q, k, v, g = jnp.split(u.astype(jnp.float32), RET_SPLITS, axis=-1)
    q = rotary(q.reshape(b_, t_, RET_HEADS, RET_DK), positions) * RET_DK ** -0.5
    k = rotary(k.reshape(b_, t_, RET_HEADS, RET_DK), positions)
    v = v.reshape(b_, t_, RET_HEADS, RET_DV)
    log_gamma = jnp.log(1.0 - jnp.exp2(-5.0 - jnp.arange(RET_HEADS, dtype=jnp.float32)))
    y = head_rmsnorm(retention_chunked(q, k, v, log_gamma)).reshape(b_, t_, MIX_WIDTH)
    return jax.nn.silu(g) * y


def setup_inputs(seed: int = 0) -> dict:
    key = jax.random.key(seed)
    keys = iter(jax.random.split(key, 40))
    nrm = lambda shape, scale: scale * jax.random.normal(next(keys), shape, jnp.float32)
    gain = lambda shape: 1.0 + nrm(shape, 0.02)
    L, D, M = DEPTH, D_MODEL, MIX_WIDTH
    inp = {}
    inp['x'] = nrm((BATCH, SEQ, D), 1.0)
    inp['p'] = nrm((DEPTH, BATCH, SEQ, PLE_DIM), 1.0)
    offs = jax.random.randint(next(keys), (BATCH, 1), 0, 1024, dtype=jnp.int32)
    inp['positions'] = offs + jnp.arange(SEQ, dtype=jnp.int32)[None, :]
    inp['mix_norm'] = gain((L, D))
    inp['w_in'] = nrm((L, D, N_IN), D ** -0.5)
    inp['gla_w_decay'] = nrm((L, GLA_LORA, GLA_QK), GLA_LORA ** -0.5)
    inp['gla_b_decay'] = nrm((L, GLA_QK), 0.1)
    inp['gla_norm'] = gain((L, GLA_DV))
    inp['rwkv_mu'] = jax.random.uniform(next(keys), (L, RWKV_COLS), jnp.float32)
    inp['rwkv_w0'] = -1.0 + nrm((L, M), 0.5)
    inp['rwkv_w2'] = nrm((L, RWKV_W_LORA, M), RWKV_W_LORA ** -0.5)
    inp['rwkv_a0'] = nrm((L, M), 0.5)
    inp['rwkv_a2'] = nrm((L, RWKV_A_LORA, M), RWKV_A_LORA ** -0.5)
    inp['rwkv_g2'] = nrm((L, RWKV_G_LORA, M), RWKV_G_LORA ** -0.5)
    inp['rwkv_k_k'] = 0.85 + nrm((L, M), 0.05)
    inp['rwkv_k_a'] = 1.0 + nrm((L, M), 0.05)
    inp['rwkv_r_k'] = nrm((L, RWKV_HEADS, RWKV_HEAD), 0.1)
    inp['rwkv_ln_w'] = gain((L, M))
    inp['rwkv_ln_b'] = nrm((L, M), 0.02)
    inp['rwkv_v0'] = nrm((L - 1, M), 0.5)
    inp['rwkv_v1'] = nrm((L - 1, M, RWKV_V_LORA), M ** -0.5)
    inp['rwkv_v2'] = nrm((L - 1, RWKV_V_LORA, M), RWKV_V_LORA ** -0.5)
    inp['w_br_gla'] = nrm((L, M, D), M ** -0.5)
    inp['w_br_rwkv'] = nrm((L, M, D), M ** -0.5)
    inp['w_br_ret'] = nrm((L, M, D), M ** -0.5)
    inp['w_o'] = nrm((L, D, D), D ** -0.5)
    inp['ffn_norm'] = gain((L, D))
    inp['w_up'] = nrm((L, D, 2 * D_FF), D ** -0.5)
    inp['conv_w'] = nrm((L, CONV_W, D_FF), CONV_W ** -0.5)
    inp['conv_b'] = nrm((L, D_FF), 0.02)
    inp['w_down'] = nrm((L, D_FF, D), D_FF ** -0.5)
    inp['ple_norm'] = gain((L, D))
    inp['w_ple_gate'] = nrm((L, D, D), D ** -0.5)
    inp['w_ple_proj'] = nrm((L, PLE_DIM, D), PLE_DIM ** -0.5)
    inp['final_norm'] = gain((D,))
    return inp


def reference(x, p, positions, mix_norm, w_in, gla_w_decay, gla_b_decay, gla_norm,
              rwkv_mu, rwkv_w0, rwkv_w2, rwkv_a0, rwkv_a2, rwkv_g2, rwkv_k_k, rwkv_k_a,
              rwkv_r_k, rwkv_ln_w, rwkv_ln_b, rwkv_v0, rwkv_v1, rwkv_v2,
              w_br_gla, w_br_rwkv, w_br_ret, w_o, ffn_norm, w_up, conv_w, conv_b, w_down,
              ple_norm, w_ple_gate, w_ple_proj, final_norm):
    v_first = None
    for i in range(DEPTH):
        h = rmsnorm(x, mix_norm[i])
        u = h @ w_in[i]
        u_gla, u_rwkv, u_ret, u_gate = jnp.split(u, IN_SPLITS, axis=-1)
        o_a = gla_branch(u_gla, gla_w_decay[i], gla_b_decay[i], gla_norm[i])
        v_mix = None if i == 0 else (rwkv_v0[i - 1], rwkv_v1[i - 1], rwkv_v2[i - 1])
        o_b, v_first = rwkv7_branch(u_rwkv, rwkv_mu[i], rwkv_w0[i], rwkv_w2[i], rwkv_a0[i],
                                    rwkv_a2[i], rwkv_g2[i], rwkv_k_k[i], rwkv_k_a[i],
                                    rwkv_r_k[i], rwkv_ln_w[i], rwkv_ln_b[i], v_first, v_mix)
        o_c = retention_branch(u_ret, positions)
        gate_a, gate_b, gate_c = jnp.split(jax.nn.sigmoid(u_gate), 3, axis=-1)
        merged = (gate_a * (o_a.astype(x.dtype) @ w_br_gla[i])
                  + gate_b * (o_b.astype(x.dtype) @ w_br_rwkv[i])
                  + gate_c * (o_c.astype(x.dtype) @ w_br_ret[i]))
        x = x + merged @ w_o[i]
        h = rmsnorm(x, ffn_norm[i])
        z, val = jnp.split(h @ w_up[i], 2, axis=-1)
        zc = lax.conv_general_dilated(z, conv_w[i][:, None, :].astype(z.dtype), window_strides=(1,),
                                      padding=[(CONV_W - 1, 0)],
                                      dimension_numbers=('NWC', 'WIO', 'NWC'),
                                      feature_group_count=D_FF) + conv_b[i]
        x = x + (jax.nn.gelu(zc, approximate=True) * val) @ w_down[i]
        h = rmsnorm(x, ple_norm[i])
        x = x + jax.nn.sigmoid(h @ w_ple_gate[i]) * (p[i] @ w_ple_proj[i])
    return rmsnorm(x, final_norm)
```

```python
import functools
import math

import numpy as np
import jax
import jax.numpy as jnp
from jax import lax
from jax.experimental import pallas as pl
from jax.experimental.pallas import tpu as pltpu

F32 = jnp.float32
BF16 = jnp.bfloat16

D_MODEL = 2048
PLE_DIM = 256
MIX = D_MODEL // 2
CHUNK = 64
EPS = 1e-6
GLA_HEADS, GLA_DK, GLA_DV, GLA_LORA = 4, 128, 256, 16
GLA_GATE_NORMALIZER = 16.0
RWKV_HEAD = 64
RWKV_PAIRS = MIX // (2 * RWKV_HEAD)
RWKV_W_LORA, RWKV_A_LORA, RWKV_V_LORA, RWKV_G_LORA = 64, 64, 32, 160
RWKV_GN_EPS = 64e-5
RET_HEADS, RET_DK, RET_DV = 4, 128, 256
ROPE_BASE = 10000.0
D_FF = 5504
CONV_W = 3
GLA_QK = GLA_HEADS * GLA_DK
RET_QK = RET_HEADS * RET_DK
GLA_COLS = 2 * GLA_QK + 2 * MIX + GLA_LORA
RWKV_COLS = 3 * MIX + RWKV_W_LORA + RWKV_A_LORA + RWKV_G_LORA
RET_COLS = 2 * RET_QK + 2 * MIX
GATE_COLS = 3 * D_MODEL

LANES = 128
VMEM_LIMIT = 56 * 1024 * 1024

SEG = 3 * MIX
N_MAIN = GATE_COLS + 3 * SEG
SM_WL, SM_AL, SM_GL, SM_LR = 0, 128, 256, 512
N_SMALL = 640
FF_PAD = 5632

TM = 512
TN_IN = 1024
TN_MERGE = 512
TN_FF = 512
ROWS = 256


def _cparams(n_axes):
    return pltpu.CompilerParams(dimension_semantics=("arbitrary",) * n_axes,
                                vmem_limit_bytes=VMEM_LIMIT)


def _dot(a, b):
    return jnp.dot(a.astype(BF16), b.astype(BF16), preferred_element_type=F32)


def _dot_nt(a, b):
    return lax.dot_general(a.astype(BF16), b.astype(BF16), (((1,), (1,)), ((), ())),
                           preferred_element_type=F32)


def _dot_tn(a, b):
    return lax.dot_general(a.astype(BF16), b.astype(BF16), (((0,), (0,)), ((), ())),
                           preferred_element_type=F32)


def _split(x, parts):
    out = []
    for _ in range(parts):
        p = x.astype(BF16)
        out.append(p)
        x = x - p.astype(F32)
    return out


def _dot_exact_rhs(x, m_bf16, parts):
    acc = None
    for p in _split(x, parts):
        t = jnp.dot(p, m_bf16, preferred_element_type=F32)
        acc = t if acc is None else acc + t
    return acc


def _dot_lhs_exact(m_bf16, x, parts):
    acc = None
    for p in _split(x, parts):
        t = jnp.dot(m_bf16, p, preferred_element_type=F32)
        acc = t if acc is None else acc + t
    return acc


def _dot3(a, b):
    ah, al = _split(a, 2)
    bh, bl = _split(b, 2)
    return (jnp.dot(ah, bh, preferred_element_type=F32)
            + (jnp.dot(ah, bl, preferred_element_type=F32)
               + jnp.dot(al, bh, preferred_element_type=F32)))


def _sigmoid(x):
    return 1.0 / (1.0 + jnp.exp(-x))


def _silu(x):
    return x * _sigmoid(x)


def _softplus(x):
    return jnp.maximum(x, 0.0) + jnp.log1p(jnp.exp(-jnp.abs(x)))


def _rms_rows(x, gain):
    return x * lax.rsqrt(jnp.mean(x * x, axis=-1, keepdims=True) + EPS) * gain


def _tri_incl(n):
    r = lax.broadcasted_iota(jnp.int32, (n, n), 0)
    c = lax.broadcasted_iota(jnp.int32, (n, n), 1)
    return r >= c


def _norm_mm_kernel(x_ref, g_ref, w_ref, o_ref, h_ref):
    @pl.when(pl.program_id(1) == 0)
    def _():
        h_ref[...] = _rms_rows(x_ref[...], g_ref[...]).astype(BF16)

    o_ref[...] = jnp.dot(h_ref[...], w_ref[...], preferred_element_type=F32).astype(o_ref.dtype)


def _norm_matmul(x, gain, w, tn, name):
    t, d = x.shape
    n = w.shape[1]
    return pl.pallas_call(
        _norm_mm_kernel,
        grid=(t // TM, n // tn),
        in_specs=[pl.BlockSpec((TM, d), lambda i, j: (i, 0)),
                  pl.BlockSpec((1, d), lambda i, j: (0, 0)),
                  pl.BlockSpec((d, tn), lambda i, j: (0, j))],
        out_specs=pl.BlockSpec((TM, tn), lambda i, j: (i, j)),
        out_shape=jax.ShapeDtypeStruct((t, n), F32),
        scratch_shapes=[pltpu.VMEM((TM, d), BF16)],
        compiler_params=_cparams(2),
        name=name,
    )(x, gain.reshape(1, d), w)


def _rope_kernel(pos_ref, f_ref, s_ref, cos_ref, sin_ref):
    ang = pos_ref[...] * f_ref[...]
    cos_ref[...] = jnp.cos(ang)
    sin_ref[...] = jnp.sin(ang) * s_ref[...]


def _rope_tables(pos):
    t = pos.shape[0]
    half = RET_DK // 2
    inv = (ROPE_BASE ** (-np.arange(0, RET_DK, 2, dtype=np.float32) / np.float32(RET_DK))).astype(np.float32)
    inv2 = jnp.asarray(np.concatenate([inv, inv]).reshape(1, RET_DK))
    sign = jnp.asarray(np.concatenate([-np.ones(half, np.float32), np.ones(half, np.float32)]).reshape(1, RET_DK))
    rows = TM
    return pl.pallas_call(
        _rope_kernel,
        grid=(t // rows,),
        in_specs=[pl.BlockSpec((rows, 1), lambda i: (i, 0)),
                  pl.BlockSpec((1, RET_DK), lambda i: (0, 0)),
                  pl.BlockSpec((1, RET_DK), lambda i: (0, 0))],
        out_specs=[pl.BlockSpec((rows, RET_DK), lambda i: (i, 0))] * 2,
        out_shape=[jax.ShapeDtypeStruct((t, RET_DK), F32)] * 2,
        compiler_params=_cparams(1),
        name="rope_tables",
    )(pos, inv2, sign)


def _gla_kernel(u_ref, lr_ref, wd_ref, bd_ref, ng_ref, o_ref, st_ref, la_ref, *, steps_per_seq):
    @pl.when(pl.program_id(0) % steps_per_seq == 0)
    def _():
        st_ref[...] = jnp.zeros_like(st_ref)

    z = jnp.dot(lr_ref[...], wd_ref[...], precision=lax.Precision.HIGHEST,
                preferred_element_type=F32) + bd_ref[...]
    la_ref[...] = -_softplus(-z) * (1.0 / GLA_GATE_NORMALIZER)

    causal = _tri_incl(CHUNK)
    tri = causal.astype(BF16)
    scale = GLA_DK ** -0.5
    ng = ng_ref[...]

    def chunk(c, carry):
        r0 = pl.multiple_of(c * CHUNK, CHUNK)
        rows = pl.ds(r0, CHUNK)
        for h in range(GLA_HEADS):
            ks = slice(h * GLA_DK, (h + 1) * GLA_DK)
            q = u_ref[rows, ks]
            k = u_ref[rows, GLA_QK + h * GLA_DK:GLA_QK + (h + 1) * GLA_DK]
            v = u_ref[rows, 2 * GLA_QK + h * GLA_DV:2 * GLA_QK + (h + 1) * GLA_DV]
            g = u_ref[rows, 2 * GLA_QK + MIX + h * GLA_DV:2 * GLA_QK + MIX + (h + 1) * GLA_DV]
            cum = _dot_lhs_exact(tri, la_ref[rows, ks], 3)
            cl = cum[CHUNK - 1:CHUNK, :]
            q_in = (q * jnp.exp(cum) * scale).astype(BF16)
            k_in = k * jnp.exp(-cum)
            k_st = k * jnp.exp(cl - cum)
            vb = v.astype(BF16)
            sc = jnp.where(causal, _dot_nt(q_in, k_in), 0.0)
            st = st_ref[h]
            o = _dot(sc, vb) + _dot_nt(q_in, st)
            st_ref[h] = st * jnp.exp(cl) + _dot_tn(vb, k_st)
            o = o * lax.rsqrt(jnp.mean(o * o, axis=-1, keepdims=True) + EPS) * ng
            o_ref[rows, h * GLA_DV:(h + 1) * GLA_DV] = (o * _silu(g)).astype(o_ref.dtype)
        return carry

    lax.fori_loop(0, ROWS // CHUNK, chunk, 0)


def _gla(u, small, wd_pad, b_decay, norm_g, seq):
    t = u.shape[0]
    return pl.pallas_call(
        functools.partial(_gla_kernel, steps_per_seq=seq // ROWS),
        grid=(t // ROWS,),
        in_specs=[pl.BlockSpec((ROWS, SEG), lambda i: (i, GATE_COLS // SEG)),
                  pl.BlockSpec((ROWS, LANES), lambda i: (i, SM_LR // LANES)),
                  pl.BlockSpec((LANES, GLA_QK), lambda i: (0, 0)),
                  pl.BlockSpec((1, GLA_QK), lambda i: (0, 0)),
                  pl.BlockSpec((1, GLA_DV), lambda i: (0, 0))],
        out_specs=pl.BlockSpec((ROWS, MIX), lambda i: (i, 0)),
        out_shape=jax.ShapeDtypeStruct((t, MIX), BF16),
        scratch_shapes=[pltpu.VMEM((GLA_HEADS, GLA_DV, GLA_DK), F32),
                        pltpu.VMEM((ROWS, GLA_QK), F32)],
        compiler_params=_cparams(1),
        name="gla_mixer",
    )(u, small, wd_pad, b_decay.reshape(1, GLA_QK), norm_g.reshape(1, GLA_DV))


def _ret_consts():
    lg = np.log(1.0 - np.exp2(-5.0 - np.arange(RET_HEADS, dtype=np.float32))).astype(np.float32)
    pos = np.arange(CHUNK, dtype=np.float32)
    diff = pos[:, None] - pos[None, :]
    causal = diff >= 0
    dmask = np.where(causal, np.exp(lg[:, None, None] * np.where(causal, diff, 0.0)), 0.0).astype(np.float32)
    qs = np.exp(lg[:, None] * (pos[None, :] + 1.0)).astype(np.float32)
    vs = np.exp(lg[:, None] * (CHUNK - 1.0 - pos[None, :])).astype(np.float32)
    cd = [float(np.exp(np.float32(l) * np.float32(CHUNK))) for l in lg]
    qs_b = np.broadcast_to(qs[:, :, None], (RET_HEADS, CHUNK, RET_DK)).copy()
    vs_b = np.broadcast_to(vs[:, :, None], (RET_HEADS, CHUNK, RET_DV)).copy()
    return dmask, qs_b, vs_b, cd


def _ret_kernel(u_ref, cos_ref, sin_ref, dm_ref, qs_ref, vs_ref, o_ref, st_ref, *, steps_per_seq, chunk_decay):
    @pl.when(pl.program_id(0) % steps_per_seq == 0)
    def _():
        st_ref[...] = jnp.zeros_like(st_ref)

    scale = RET_DK ** -0.5
    half = RET_DK // 2

    def chunk(c, carry):
        r0 = pl.multiple_of(c * CHUNK, CHUNK)
        rows = pl.ds(r0, CHUNK)
        cos = cos_ref[rows, :]
        sin = sin_ref[rows, :]
        for h in range(RET_HEADS):
            q = u_ref[rows, h * RET_DK:(h + 1) * RET_DK]
            k = u_ref[rows, RET_QK + h * RET_DK:RET_QK + (h + 1) * RET_DK]
            v = u_ref[rows, 2 * RET_QK + h * RET_DV:2 * RET_QK + (h + 1) * RET_DV]
            g = u_ref[rows, 2 * RET_QK + MIX + h * RET_DV:2 * RET_QK + MIX + (h + 1) * RET_DV]
            q = (q * cos + pltpu.roll(q, half, 1) * sin) * scale
            k = (k * cos + pltpu.roll(k, half, 1) * sin).astype(BF16)
            sc = _dot_nt(q, k) * dm_ref[h]
            st = st_ref[h]
            o = _dot(sc, v) + _dot_nt(q * qs_ref[h], st)
            st_ref[h] = st * chunk_decay[h] + _dot_tn(v * vs_ref[h], k)
            y = o * lax.rsqrt(jnp.mean(o * o, axis=-1, keepdims=True) + EPS)
            o_ref[rows, h * RET_DV:(h + 1) * RET_DV] = (_silu(g) * y).astype(o_ref.dtype)
        return carry

    lax.fori_loop(0, ROWS // CHUNK, chunk, 0)


def _retention(u, cos2, sin2, seq):
    t = u.shape[0]
    dmask, qs_b, vs_b, cd = _ret_consts()
    return pl.pallas_call(
        functools.partial(_ret_kernel, steps_per_seq=seq // ROWS, chunk_decay=tuple(cd)),
        grid=(t // ROWS,),
        in_specs=[pl.BlockSpec((ROWS, SEG), lambda i: (i, GATE_COLS // SEG + 2)),
                  pl.BlockSpec((ROWS, RET_DK), lambda i: (i, 0)),
                  pl.BlockSpec((ROWS, RET_DK), lambda i: (i, 0)),
                  pl.BlockSpec((RET_HEADS, CHUNK, CHUNK), lambda i: (0, 0, 0)),
                  pl.BlockSpec((RET_HEADS, CHUNK, RET_DK), lambda i: (0, 0, 0)),
                  pl.BlockSpec((RET_HEADS, CHUNK, RET_DV), lambda i: (0, 0, 0))],
        out_specs=pl.BlockSpec((ROWS, MIX), lambda i: (i, 0)),
        out_shape=jax.ShapeDtypeStruct((t, MIX), BF16),
        scratch_shapes=[pltpu.VMEM((RET_HEADS, RET_DV, RET_DK), F32)],
        compiler_params=_cparams(1),
        name="retention_mixer",
    )(u, cos2, sin2, jnp.asarray(dmask), jnp.asarray(qs_b), jnp.asarray(vs_b))


def _rwkv_kernel(*refs, steps_per_seq, first_layer):
    if first_layer:
        (u_ref, sm_ref, mu_ref, mus_ref, w0_ref, w2_ref, a0_ref, a2_ref, g2_ref, kk_ref, ka_ref,
         rk_ref, lnw_ref, lnb_ref, o_ref, vf_out_ref,
         st_ref, cu_ref, cs_ref, r_s, lw_s, k_s, v_s, kk_s, a_s, g_s) = refs
    else:
        (u_ref, sm_ref, mu_ref, mus_ref, w0_ref, w2_ref, a0_ref, a2_ref, g2_ref, kk_ref, ka_ref,
         rk_ref, lnw_ref, lnb_ref, vf_ref, v0_ref, v1_ref, v2_ref, o_ref,
         st_ref, cu_ref, cs_ref, r_s, lw_s, k_s, v_s, kk_s, a_s, g_s) = refs

    @pl.when(pl.program_id(0) % steps_per_seq == 0)
    def _():
        st_ref[...] = jnp.zeros_like(st_ref)
        cu_ref[...] = jnp.zeros_like(cu_ref)
        cs_ref[...] = jnp.zeros_like(cs_ref)

    row = lax.broadcasted_iota(jnp.int32, (ROWS, 1), 0)

    def shifted(x, carry_ref, mu):
        prev = jnp.where(row == 0, carry_ref[0:1, :], pltpu.roll(x, 1, 0))
        carry_ref[0:1, :] = x[ROWS - 1:ROWS, :]
        return x + (prev - x) * mu

    us = shifted(u_ref[...], cu_ref, mu_ref[...])
    ss = shifted(sm_ref[...], cs_ref, mus_ref[...])
    r = us[:, 0:MIX]
    k = us[:, MIX:2 * MIX]
    v = us[:, 2 * MIX:3 * MIX]
    wl = ss[:, SM_WL:SM_WL + LANES]
    al = ss[:, SM_AL:SM_AL + LANES]
    gl = ss[:, SM_GL:SM_GL + 2 * LANES]
    w = w0_ref[...] + _dot(jnp.tanh(wl), w2_ref[...])
    lw_s[...] = -jnp.exp(-_softplus(-w) - 0.5)
    a = _sigmoid(a0_ref[...] + _dot(al, a2_ref[...]))
    g_s[...] = _dot(_sigmoid(gl), g2_ref[...])
    if first_layer:
        vf_out_ref[...] = v
    else:
        mix = _sigmoid(v0_ref[...] + _dot(_dot(v, v1_ref[...]), v2_ref[...]))
        v = v + (vf_ref[...] - v) * mix
    r_s[...] = r
    v_s[...] = v
    a_s[...] = a
    kk_s[...] = k * kk_ref[...]
    k_s[...] = k * (1.0 + (a - 1.0) * ka_ref[...])

    lane = lax.broadcasted_iota(jnp.int32, (1, LANES), 1)
    mlo = (lane < RWKV_HEAD).astype(F32)
    mhi = 1.0 - mlo
    rr = lax.broadcasted_iota(jnp.int32, (LANES, LANES), 0)
    cc = lax.broadcasted_iota(jnp.int32, (LANES, LANES), 1)
    bd_mask = (rr // RWKV_HEAD) == (cc // RWKV_HEAD)
    bd = bd_mask.astype(BF16)
    eye = (rr == cc).astype(F32)
    tri = _tri_incl(CHUNK).astype(BF16)
    trow = lax.broadcasted_iota(jnp.int32, (CHUNK, LANES), 0)
    scol = lax.broadcasted_iota(jnp.int32, (CHUNK, LANES), 1) % RWKV_HEAD
    strict = scol < trow
    incl = scol <= trow
    inv_n = 1.0 / RWKV_HEAD

    def stack(x):
        return jnp.concatenate([x * mlo, x * mhi], axis=0)

    def chunk(c, carry):
        r0 = pl.multiple_of(c * CHUNK, CHUNK)
        rows = pl.ds(r0, CHUNK)
        for p in range(RWKV_PAIRS):
            cols = slice(p * LANES, (p + 1) * LANES)
            r_t, lw_t, k_t, v_t = r_s[rows, cols], lw_s[rows, cols], k_s[rows, cols], v_s[rows, cols]
            kkr, a_t, g_t = kk_s[rows, cols], a_s[rows, cols], g_s[rows, cols]
            nrm = jnp.sqrt(_dot_exact_rhs(kkr * kkr, bd, 2))
            kk = kkr / jnp.maximum(nrm, 1e-12)
            bv = kk * a_t
            cum = _dot_lhs_exact(tri, lw_t, 3)
            cl = cum[CHUNK - 1:CHUNK, :]
            e_in = jnp.exp(cum)
            e_out = jnp.exp(-cum)
            e_end = jnp.exp(cl - cum)
            at = -kk * jnp.exp(cum - lw_t)
            rt = r_t * e_in
            lhs = jnp.concatenate([stack(at), stack(rt)], axis=0)
            rhs = jnp.concatenate([bv * e_out, k_t * e_out], axis=0)
            gm = _dot_nt(lhs, rhs)
            g0 = jnp.where(strict, gm[0:CHUNK], 0.0)
            g1 = jnp.where(strict, gm[CHUNK:2 * CHUNK], 0.0)
            m0 = jnp.where(incl, gm[2 * CHUNK:3 * CHUNK], 0.0)
            m1 = jnp.where(incl, gm[3 * CHUNK:4 * CHUNK], 0.0)
            g0r, g1r = pltpu.roll(g0, RWKV_HEAD, 1), pltpu.roll(g1, RWKV_HEAD, 1)
            m0r, m1r = pltpu.roll(m0, RWKV_HEAD, 1), pltpu.roll(m1, RWKV_HEAD, 1)
            ak = g0r * mlo + g1 * mhi
            rb = m0 * mlo + m1r * mhi
            rk = m0r * mlo + m1 * mhi
            n1 = jnp.concatenate([g0 * mlo, g1r * mhi], axis=0)
            tinv = eye + n1
            npow = n1
            for _ in range(int(math.log2(CHUNK)) - 1):
                npow = _dot3(npow, npow)
                tinv = tinv + _dot3(tinv, npow)
            tcat = tinv[0:CHUNK] + tinv[CHUNK:2 * CHUNK]
            st = st_ref[p]
            vs2 = stack(v_t)
            u_rhs = _dot_nt(at, st) + _dot(ak, vs2)
            u_t = _dot(tcat, stack(u_rhs))
            y = _dot_nt(rt, st) + _dot(rb, stack(u_t)) + _dot(rk, vs2)
            upd = _dot_tn(jnp.concatenate([u_t, v_t], axis=0),
                          jnp.concatenate([bv * e_end, k_t * e_end], axis=0))
            st_ref[p] = st * jnp.exp(cl) + jnp.where(bd_mask, upd, 0.0)
            mean = _dot_exact_rhs(y, bd, 2) * inv_n
            d = y - mean
            var = _dot_exact_rhs(d * d, bd, 2) * inv_n
            yn = d * lax.rsqrt(var + RWKV_GN_EPS) * lnw_ref[:, cols] + lnb_ref[:, cols]
            bonus = _dot_exact_rhs(r_t * k_t * rk_ref[:, cols], bd, 2) * v_t
            o_ref[rows, cols] = ((yn + bonus) * g_t).astype(o_ref.dtype)
        return carry

    lax.fori_loop(0, ROWS // CHUNK, chunk, 0)


def _rwkv(u, small, prm, v_first, seq):
    t = u.shape[0]
    first = v_first is None
    row = lambda a: a.reshape(1, -1)
    vec = lambda n: pl.BlockSpec((1, n), lambda i: (0, 0))
    mat = lambda r, c: pl.BlockSpec((r, c), lambda i: (0, 0))
    args = [u, small, row(prm["mu"]), row(prm["mu_s"]), row(prm["w0"]), prm["w2"], row(prm["a0"]), prm["a2"],
            prm["g2"], row(prm["k_k"]), row(prm["k_a"]), row(prm["r_k"]), row(prm["ln_w"]), row(prm["ln_b"])]
    in_specs = [pl.BlockSpec((ROWS, SEG), lambda i: (i, GATE_COLS // SEG + 1)),
                pl.BlockSpec((ROWS, 4 * LANES), lambda i: (i, 0)),
                vec(SEG), vec(4 * LANES), vec(MIX), mat(LANES, MIX), vec(MIX), mat(LANES, MIX),
                mat(2 * LANES, MIX), vec(MIX), vec(MIX), vec(MIX), vec(MIX), vec(MIX)]
    out_block = pl.BlockSpec((ROWS, MIX), lambda i: (i, 0))
    if first:
        out_specs = [out_block, out_block]
        out_shape = [jax.ShapeDtypeStruct((t, MIX), BF16), jax.ShapeDtypeStruct((t, MIX), F32)]
    else:
        args += [v_first, row(prm["v0"]), prm["v1"], prm["v2"]]
        in_specs += [out_block, vec(MIX), mat(MIX, LANES), mat(LANES, MIX)]
        out_specs = out_block
        out_shape = jax.ShapeDtypeStruct((t, MIX), BF16)
    tok = pltpu.VMEM((ROWS, MIX), F32)
    res = pl.pallas_call(
        functools.partial(_rwkv_kernel, steps_per_seq=seq // ROWS, first_layer=first),
        grid=(t // ROWS,),
        in_specs=in_specs,
        out_specs=out_specs,
        out_shape=out_shape,
        scratch_shapes=[pltpu.VMEM((RWKV_PAIRS, LANES, LANES), F32),
                        pltpu.VMEM((8, SEG), F32), pltpu.VMEM((8, 4 * LANES), F32),
                        tok, tok, tok, tok, tok, tok, tok],
        compiler_params=_cparams(1),
        name="rwkv7_mixer",
    )(*args)
    return (res[0], res[1]) if first else (res, v_first)


def _merge_kernel(x_ref, oa_ref, ob_ref, oc_ref, ga_ref, gb_ref, gc_ref, wa_ref, wb_ref, wc_ref,
                  wo_ref, o_ref):
    @pl.when(pl.program_id(1) == 0)
    def _():
        o_ref[...] = x_ref[...]

    m = (_sigmoid(ga_ref[...]) * jnp.dot(oa_ref[...], wa_ref[...], preferred_element_type=F32)
         + _sigmoid(gb_ref[...]) * jnp.dot(ob_ref[...], wb_ref[...], preferred_element_type=F32)
         + _sigmoid(gc_ref[...]) * jnp.dot(oc_ref[...], wc_ref[...], preferred_element_type=F32))
    o_ref[...] += jnp.dot(m.astype(BF16), wo_ref[...], preferred_element_type=F32)


def _merge_out(x, u, o_a, o_b, o_c, wa, wb, wc, wo):
    t, d = x.shape
    nj = d // TN_MERGE
    full = pl.BlockSpec((TM, d), lambda i, j: (i, 0))
    mixo = pl.BlockSpec((TM, MIX), lambda i, j: (i, 0))
    gate = lambda g: pl.BlockSpec((TM, TN_MERGE), lambda i, j: (i, g * nj + j))
    wbr = pl.BlockSpec((MIX, TN_MERGE), lambda i, j: (0, j))
    return pl.pallas_call(
        _merge_kernel,
        grid=(t // TM, nj),
        in_specs=[full, mixo, mixo, mixo, gate(0), gate(1), gate(2), wbr, wbr, wbr,
                  pl.BlockSpec((TN_MERGE, d), lambda i, j: (j, 0))],
        out_specs=full,
        out_shape=jax.ShapeDtypeStruct((t, d), F32),
        compiler_params=_cparams(2),
        name="merge_out_proj",
    )(x, o_a, o_b, o_c, u, u, u, wa, wb, wc, wo)


def _ffn_kernel(x_ref, g_ref, wz_ref, wv_ref, cw_ref, cb_ref, wd_ref, o_ref, h_ref, carry_ref, *,
                tiles_per_seq):
    i, j = pl.program_id(0), pl.program_id(1)

    @pl.when(j == 0)
    def _():
        x = x_ref[...]
        h_ref[...] = _rms_rows(x, g_ref[...]).astype(BF16)
        o_ref[...] = x

    @pl.when(jnp.logical_and(j == 0, i % tiles_per_seq == 0))
    def _():
        carry_ref[...] = jnp.zeros_like(carry_ref)

    h = h_ref[...]
    z = jnp.dot(h, wz_ref[...], preferred_element_type=F32)
    val = jnp.dot(h, wv_ref[...], preferred_element_type=F32)
    slot = pl.ds(pl.multiple_of(j * 8, 8), 8)
    tail = carry_ref[slot, :]
    carry_ref[slot, :] = z[TM - 8:TM, :]
    row = lax.broadcasted_iota(jnp.int32, (TM, 1), 0)
    z1 = jnp.where(row == 0, tail[7:8, :], pltpu.roll(z, 1, 0))
    z2 = jnp.where(row == 0, tail[6:7, :], jnp.where(row == 1, tail[7:8, :], pltpu.roll(z, 2, 0)))
    cw = cw_ref[...]
    zc = z * cw[2:3, :] + z1 * cw[1:2, :] + z2 * cw[0:1, :] + cb_ref[...]
    gelu = 0.5 * zc * (1.0 + jnp.tanh(0.7978845608028654 * (zc + 0.044715 * (zc * zc * zc))))
    o_ref[...] += jnp.dot((gelu * val).astype(BF16), wd_ref[...], preferred_element_type=F32)


def _ffn(x, gain, wz, wv, cw, cb, wd, seq):
    t, d = x.shape
    nj = FF_PAD // TN_FF
    full = pl.BlockSpec((TM, d), lambda i, j: (i, 0))
    return pl.pallas_call(
        functools.partial(_ffn_kernel, tiles_per_seq=seq // TM),
        grid=(t // TM, nj),
        in_specs=[full, pl.BlockSpec((1, d), lambda i, j: (0, 0)),
                  pl.BlockSpec((d, TN_FF), lambda i, j: (0, j)),
                  pl.BlockSpec((d, TN_FF), lambda i, j: (0, j)),
                  pl.BlockSpec((CONV_W, TN_FF), lambda i, j: (0, j)),
                  pl.BlockSpec((1, TN_FF), lambda i, j: (0, j)),
                  pl.BlockSpec((TN_FF, d), lambda i, j: (j, 0))],
        out_specs=full,
        out_shape=jax.ShapeDtypeStruct((t, d), F32),
        scratch_shapes=[pltpu.VMEM((TM, d), BF16), pltpu.VMEM((nj * 8, TN_FF), F32)],
        compiler_params=_cparams(2),
        name="ffn_geglu_conv",
    )(x, gain.reshape(1, d), wz, wv, cw, cb.reshape(1, FF_PAD), wd)


def _ple_kernel(x_ref, g_ref, wg_ref, p_ref, wp_ref, fg_ref, o_ref, *, final):
    x = x_ref[...]
    h = _rms_rows(x, g_ref[...])
    gate = _sigmoid(_dot(h, wg_ref[...]))
    y = x + gate * _dot(p_ref[...], wp_ref[...])
    if final:
        y = _rms_rows(y, fg_ref[...])
    o_ref[...] = y


def _ple(x, gain, wg, p, wp, final_gain, final):
    t, d = x.shape
    full = pl.BlockSpec((TM, d), lambda i: (i, 0))
    vec = pl.BlockSpec((1, d), lambda i: (0, 0))
    return pl.pallas_call(
        functools.partial(_ple_kernel, final=final),
        grid=(t // TM,),
        in_specs=[full, vec, pl.BlockSpec((d, d), lambda i: (0, 0)),
                  pl.BlockSpec((TM, PLE_DIM), lambda i: (i, 0)),
                  pl.BlockSpec((PLE_DIM, d), lambda i: (0, 0)), vec],
        out_specs=full,
        out_shape=jax.ShapeDtypeStruct((t, d), F32),
        compiler_params=_cparams(1),
        name="ple_embed",
    )(x, gain.reshape(1, d), wg, p, wp, final_gain.reshape(1, d))


def _pad_cols(a, n):
    return jnp.pad(a, ((0, 0), (0, n - a.shape[1])))


def _pad_rows(a, n):
    return jnp.pad(a, ((0, n - a.shape[0]), (0, 0)))


def _pad_vec(a, n):
    return jnp.pad(a, (0, n - a.shape[0]))


def _layout_w_in(w):
    o_rwkv = GLA_COLS
    o_ret = GLA_COLS + RWKV_COLS
    o_gate = o_ret + RET_COLS
    main = jnp.concatenate([w[:, o_gate:o_gate + GATE_COLS], w[:, 0:SEG],
                            w[:, o_rwkv:o_rwkv + SEG], w[:, o_ret:o_ret + SEG]], axis=1)
    o = o_rwkv + SEG
    small = jnp.concatenate([
        _pad_cols(w[:, o:o + RWKV_W_LORA], LANES),
        _pad_cols(w[:, o + RWKV_W_LORA:o + RWKV_W_LORA + RWKV_A_LORA], LANES),
        _pad_cols(w[:, o + RWKV_W_LORA + RWKV_A_LORA:o + RWKV_W_LORA + RWKV_A_LORA + RWKV_G_LORA], 2 * LANES),
        _pad_cols(w[:, SEG:SEG + GLA_LORA], LANES)], axis=1)
    return main.astype(BF16), small.astype(BF16)


def _layout_mu(mu):
    o = SEG
    small = jnp.concatenate([
        _pad_vec(mu[o:o + RWKV_W_LORA], LANES),
        _pad_vec(mu[o + RWKV_W_LORA:o + RWKV_W_LORA + RWKV_A_LORA], LANES),
        _pad_vec(mu[o + RWKV_W_LORA + RWKV_A_LORA:], 2 * LANES)])
    return mu[:SEG], small


def kernel(x, p, positions, mix_norm, w_in, gla_w_decay, gla_b_decay, gla_norm, rwkv_mu, rwkv_w0, rwkv_w2, rwkv_a0, rwkv_a2, rwkv_g2, rwkv_k_k, rwkv_k_a, rwkv_r_k, rwkv_ln_w, rwkv_ln_b, rwkv_v0, rwkv_v1, rwkv_v2, w_br_gla, w_br_rwkv, w_br_ret, w_o, ffn_norm, w_up, conv_w, conv_b, w_down, ple_norm, w_ple_gate, w_ple_proj, final_norm):
    b, s, d = x.shape
    depth = w_in.shape[0]
    t = b * s
    assert d == D_MODEL and s % TM == 0 and s % ROWS == 0
    xf = x.reshape(t, d)
    cos2, sin2 = _rope_tables(positions.astype(F32).reshape(t, 1))
    v_first = None
    for i in range(depth):
        w_main, w_small = _layout_w_in(w_in[i])
        u = _norm_matmul(xf, mix_norm[i], w_main, TN_IN, "in_proj_main")
        small = _norm_matmul(xf, mix_norm[i], w_small, N_SMALL, "in_proj_small")
        o_a = _gla(u, small, _pad_rows(gla_w_decay[i], LANES), gla_b_decay[i], gla_norm[i], s)
        mu_main, mu_small = _layout_mu(rwkv_mu[i])
        prm = dict(mu=mu_main, mu_s=mu_small, w0=rwkv_w0[i], w2=_pad_rows(rwkv_w2[i], LANES).astype(BF16),
                   a0=rwkv_a0[i], a2=_pad_rows(rwkv_a2[i], LANES).astype(BF16),
                   g2=_pad_rows(rwkv_g2[i], 2 * LANES).astype(BF16), k_k=rwkv_k_k[i], k_a=rwkv_k_a[i],
                   r_k=rwkv_r_k[i].reshape(MIX), ln_w=rwkv_ln_w[i], ln_b=rwkv_ln_b[i])
        if i > 0:
            prm.update(v0=rwkv_v0[i - 1], v1=_pad_cols(rwkv_v1[i - 1], LANES).astype(BF16),
                       v2=_pad_rows(rwkv_v2[i - 1], LANES).astype(BF16))
        o_b, v_first = _rwkv(u, small, prm, v_first, s)
        o_c = _retention(u, cos2, sin2, s)
        xf = _merge_out(xf, u, o_a, o_b, o_c, w_br_gla[i].astype(BF16), w_br_rwkv[i].astype(BF16),
                        w_br_ret[i].astype(BF16), w_o[i].astype(BF16))
        wz = _pad_cols(w_up[i][:, :D_FF], FF_PAD).astype(BF16)
        wv = _pad_cols(w_up[i][:, D_FF:], FF_PAD).astype(BF16)
        xf = _ffn(xf, ffn_norm[i], wz, wv, _pad_cols(conv_w[i], FF_PAD), _pad_vec(conv_b[i], FF_PAD),
                  _pad_rows(w_down[i], FF_PAD).astype(BF16), s)
        xf = _ple(xf, ple_norm[i], w_ple_gate[i].astype(BF16), p[i].reshape(t, PLE_DIM),
                  w_ple_proj[i].astype(BF16), final_norm, final=(i == depth - 1))
    return xf.reshape(b, s, d)
```

```python
import functools
import math

import numpy as np
import jax
import jax.numpy as jnp
from jax import lax
from jax.experimental import pallas as pl
from jax.experimental.pallas import tpu as pltpu

F32 = jnp.float32
BF16 = jnp.bfloat16

D_MODEL = 2048
PLE_DIM = 256
MIX = D_MODEL // 2
CHUNK = 64
EPS = 1e-6
GLA_HEADS, GLA_DK, GLA_DV, GLA_LORA = 4, 128, 256, 16
GLA_GATE_NORMALIZER = 16.0
RWKV_HEAD = 64
RWKV_PAIRS = MIX // (2 * RWKV_HEAD)
RWKV_W_LORA, RWKV_A_LORA, RWKV_V_LORA, RWKV_G_LORA = 64, 64, 32, 160
RWKV_GN_EPS = 64e-5
RET_HEADS, RET_DK, RET_DV = 4, 128, 256
ROPE_BASE = 10000.0
D_FF = 5504
CONV_W = 3
GLA_QK = GLA_HEADS * GLA_DK
RET_QK = RET_HEADS * RET_DK
GLA_COLS = 2 * GLA_QK + 2 * MIX + GLA_LORA
RWKV_COLS = 3 * MIX + RWKV_W_LORA + RWKV_A_LORA + RWKV_G_LORA
RET_COLS = 2 * RET_QK + 2 * MIX
GATE_COLS = 3 * D_MODEL

LANES = 128
VMEM_LIMIT = 56 * 1024 * 1024

SEG = 3 * MIX
N_MAIN = GATE_COLS + 3 * SEG
SM_WL, SM_AL, SM_GL, SM_LR = 0, 128, 256, 512
N_SMALL = 640
FF_PAD = 5632

TM = 512
TN_IN = 1024
TN_MERGE = 512
TN_FF = 512
ROWS = 256


def _cparams(n_axes):
    return pltpu.CompilerParams(dimension_semantics=("arbitrary",) * n_axes,
                                vmem_limit_bytes=VMEM_LIMIT)


def _dot(a, b):
    return jnp.dot(a.astype(BF16), b.astype(BF16), preferred_element_type=F32)


def _dot_nt(a, b):
    return lax.dot_general(a.astype(BF16), b.astype(BF16), (((1,), (1,)), ((), ())),
                           preferred_element_type=F32)


def _dot_tn(a, b):
    return lax.dot_general(a.astype(BF16), b.astype(BF16), (((0,), (0,)), ((), ())),
                           preferred_element_type=F32)


def _split(x, parts):
    out = []
    for _ in range(parts):
        p = x.astype(BF16)
        out.append(p)
        x = x - p.astype(F32)
    return out


def _dot_exact_rhs(x, m_bf16, parts):
    acc = None
    for p in _split(x, parts):
        t = jnp.dot(p, m_bf16, preferred_element_type=F32)
        acc = t if acc is None else acc + t
    return acc


def _dot_lhs_exact(m_bf16, x, parts):
    acc = None
    for p in _split(x, parts):
        t = jnp.dot(m_bf16, p, preferred_element_type=F32)
        acc = t if acc is None else acc + t
    return acc


def _dot3(a, b):
    ah, al = _split(a, 2)
    bh, bl = _split(b, 2)
    return (jnp.dot(ah, bh, preferred_element_type=F32)
            + (jnp.dot(ah, bl, preferred_element_type=F32)
               + jnp.dot(al, bh, preferred_element_type=F32)))


def _sigmoid(x):
    return 1.0 / (1.0 + jnp.exp(-x))


def _silu(x):
    return x * _sigmoid(x)


def _softplus(x):
    return jnp.maximum(x, 0.0) + jnp.log1p(jnp.exp(-jnp.abs(x)))


def _rms_rows(x, gain):
    return x * lax.rsqrt(jnp.mean(x * x, axis=-1, keepdims=True) + EPS) * gain


def _tri_incl(n):
    r = lax.broadcasted_iota(jnp.int32, (n, n), 0)
    c = lax.broadcasted_iota(jnp.int32, (n, n), 1)
    return r >= c


def _norm_mm_kernel(x_ref, g_ref, w_ref, o_ref, h_ref):
    @pl.when(pl.program_id(1) == 0)
    def _():
        h_ref[...] = _rms_rows(x_ref[...], g_ref[...]).astype(BF16)

    o_ref[...] = jnp.dot(h_ref[...], w_ref[...], preferred_element_type=F32).astype(o_ref.dtype)


def _norm_matmul(x, gain, w, tn, name):
    t, d = x.shape
    n = w.shape[1]
    return pl.pallas_call(
        _norm_mm_kernel,
        grid=(t // TM, n // tn),
        in_specs=[pl.BlockSpec((TM, d), lambda i, j: (i, 0)),
                  pl.BlockSpec((1, d), lambda i, j: (0, 0)),
                  pl.BlockSpec((d, tn), lambda i, j: (0, j))],
        out_specs=pl.BlockSpec((TM, tn), lambda i, j: (i, j)),
        out_shape=jax.ShapeDtypeStruct((t, n), F32),
        scratch_shapes=[pltpu.VMEM((TM, d), BF16)],
        compiler_params=_cparams(2),
        name=name,
    )(x, gain.reshape(1, d), w)


def _rope_kernel(pos_ref, f_ref, s_ref, cos_ref, sin_ref):
    ang = pos_ref[...] * f_ref[...]
    cos_ref[...] = jnp.cos(ang)
    sin_ref[...] = jnp.sin(ang) * s_ref[...]


def _rope_tables(pos):
    t = pos.shape[0]
    half = RET_DK // 2
    inv = (ROPE_BASE ** (-np.arange(0, RET_DK, 2, dtype=np.float32) / np.float32(RET_DK))).astype(np.float32)
    inv2 = jnp.asarray(np.concatenate([inv, inv]).reshape(1, RET_DK))
    sign = jnp.asarray(np.concatenate([-np.ones(half, np.float32), np.ones(half, np.float32)]).reshape(1, RET_DK))
    rows = TM
    return pl.pallas_call(
        _rope_kernel,
        grid=(t // rows,),
        in_specs=[pl.BlockSpec((rows, 1), lambda i: (i, 0)),
                  pl.BlockSpec((1, RET_DK), lambda i: (0, 0)),
                  pl.BlockSpec((1, RET_DK), lambda i: (0, 0))],
        out_specs=[pl.BlockSpec((rows, RET_DK), lambda i: (i, 0))] * 2,
        out_shape=[jax.ShapeDtypeStruct((t, RET_DK), F32)] * 2,
        compiler_params=_cparams(1),
        name="rope_tables",
    )(pos, inv2, sign)


def _gla_kernel(u_ref, lr_ref, wd_ref, bd_ref, ng_ref, o_ref, st_ref, la_ref, *, steps_per_seq):
    @pl.when(pl.program_id(0) % steps_per_seq == 0)
    def _():
        st_ref[...] = jnp.zeros_like(st_ref)

    z = jnp.dot(lr_ref[...], wd_ref[...], precision=lax.Precision.HIGHEST,
                preferred_element_type=F32) + bd_ref[...]
    la_ref[...] = -_softplus(-z) * (1.0 / GLA_GATE_NORMALIZER)

    causal = _tri_incl(CHUNK)
    tri = causal.astype(BF16)
    scale = GLA_DK ** -0.5
    ng = ng_ref[...]

    heads = range(GLA_HEADS)
    kcol = [slice(h * GLA_DK, (h + 1) * GLA_DK) for h in heads]
    vcol = [slice(h * GLA_DV, (h + 1) * GLA_DV) for h in heads]

    def chunk(c):
        rows = slice(c * CHUNK, (c + 1) * CHUNK)
        seg = lambda off, col: [u_ref[rows, off + col[h].start:off + col[h].stop] for h in heads]
        q, k = seg(0, kcol), seg(GLA_QK, kcol)
        v, g = seg(2 * GLA_QK, vcol), seg(2 * GLA_QK + MIX, vcol)
        cum = [_dot_lhs_exact(tri, la_ref[rows, kcol[h]], 3) for h in heads]
        cl = [cum[h][CHUNK - 1:CHUNK, :] for h in heads]
        q_in = [(q[h] * jnp.exp(cum[h]) * scale).astype(BF16) for h in heads]
        k_in = [k[h] * jnp.exp(-cum[h]) for h in heads]
        k_st = [k[h] * jnp.exp(cl[h] - cum[h]) for h in heads]
        vb = [v[h].astype(BF16) for h in heads]
        sc = [jnp.where(causal, _dot_nt(q_in[h], k_in[h]), 0.0) for h in heads]
        st = [st_ref[h] for h in heads]
        o = [_dot(sc[h], vb[h]) + _dot_nt(q_in[h], st[h]) for h in heads]
        upd = [_dot_tn(vb[h], k_st[h]) for h in heads]
        for h in heads:
            st_ref[h] = st[h] * jnp.exp(cl[h]) + upd[h]
        for h in heads:
            y = o[h] * lax.rsqrt(jnp.mean(o[h] * o[h], axis=-1, keepdims=True) + EPS) * ng
            o_ref[rows, vcol[h]] = (y * _silu(g[h])).astype(o_ref.dtype)

    for c in range(ROWS // CHUNK):
        chunk(c)


def _gla(u, small, wd_pad, b_decay, norm_g, seq):
    t = u.shape[0]
    return pl.pallas_call(
        functools.partial(_gla_kernel, steps_per_seq=seq // ROWS),
        grid=(t // ROWS,),
        in_specs=[pl.BlockSpec((ROWS, SEG), lambda i: (i, GATE_COLS // SEG)),
                  pl.BlockSpec((ROWS, LANES), lambda i: (i, SM_LR // LANES)),
                  pl.BlockSpec((LANES, GLA_QK), lambda i: (0, 0)),
                  pl.BlockSpec((1, GLA_QK), lambda i: (0, 0)),
                  pl.BlockSpec((1, GLA_DV), lambda i: (0, 0))],
        out_specs=pl.BlockSpec((ROWS, MIX), lambda i: (i, 0)),
        out_shape=jax.ShapeDtypeStruct((t, MIX), BF16),
        scratch_shapes=[pltpu.VMEM((GLA_HEADS, GLA_DV, GLA_DK), F32),
                        pltpu.VMEM((ROWS, GLA_QK), F32)],
        compiler_params=_cparams(1),
        name="gla_mixer",
    )(u, small, wd_pad, b_decay.reshape(1, GLA_QK), norm_g.reshape(1, GLA_DV))


def _ret_consts():
    lg = np.log(1.0 - np.exp2(-5.0 - np.arange(RET_HEADS, dtype=np.float32))).astype(np.float32)
    pos = np.arange(CHUNK, dtype=np.float32)
    diff = pos[:, None] - pos[None, :]
    causal = diff >= 0
    dmask = np.where(causal, np.exp(lg[:, None, None] * np.where(causal, diff, 0.0)), 0.0).astype(np.float32)
    qs = np.exp(lg[:, None] * (pos[None, :] + 1.0)).astype(np.float32)
    vs = np.exp(lg[:, None] * (CHUNK - 1.0 - pos[None, :])).astype(np.float32)
    cd = [float(np.exp(np.float32(l) * np.float32(CHUNK))) for l in lg]
    qs_b = np.broadcast_to(qs[:, :, None], (RET_HEADS, CHUNK, RET_DK)).copy()
    vs_b = np.broadcast_to(vs[:, :, None], (RET_HEADS, CHUNK, RET_DV)).copy()
    return dmask, qs_b, vs_b, cd


def _ret_kernel(u_ref, cos_ref, sin_ref, dm_ref, qs_ref, vs_ref, o_ref, st_ref, *, steps_per_seq, chunk_decay):
    @pl.when(pl.program_id(0) % steps_per_seq == 0)
    def _():
        st_ref[...] = jnp.zeros_like(st_ref)

    scale = RET_DK ** -0.5
    half = RET_DK // 2

    heads = range(RET_HEADS)
    kcol = [slice(h * RET_DK, (h + 1) * RET_DK) for h in heads]
    vcol = [slice(h * RET_DV, (h + 1) * RET_DV) for h in heads]

    def chunk(c):
        rows = slice(c * CHUNK, (c + 1) * CHUNK)
        cos = cos_ref[rows, :]
        sin = sin_ref[rows, :]
        seg = lambda off, col: [u_ref[rows, off + col[h].start:off + col[h].stop] for h in heads]
        q, k = seg(0, kcol), seg(RET_QK, kcol)
        v, g = seg(2 * RET_QK, vcol), seg(2 * RET_QK + MIX, vcol)
        q = [(q[h] * cos + pltpu.roll(q[h], half, 1) * sin) * scale for h in heads]
        k = [(k[h] * cos + pltpu.roll(k[h], half, 1) * sin).astype(BF16) for h in heads]
        sc = [_dot_nt(q[h], k[h]) * dm_ref[h] for h in heads]
        st = [st_ref[h] for h in heads]
        o = [_dot(sc[h], v[h]) + _dot_nt(q[h] * qs_ref[h], st[h]) for h in heads]
        upd = [_dot_tn(v[h] * vs_ref[h], k[h]) for h in heads]
        for h in heads:
            st_ref[h] = st[h] * chunk_decay[h] + upd[h]
        for h in heads:
            y = o[h] * lax.rsqrt(jnp.mean(o[h] * o[h], axis=-1, keepdims=True) + EPS)
            o_ref[rows, vcol[h]] = (_silu(g[h]) * y).astype(o_ref.dtype)

    for c in range(ROWS // CHUNK):
        chunk(c)


def _retention(u, cos2, sin2, seq):
    t = u.shape[0]
    dmask, qs_b, vs_b, cd = _ret_consts()
    return pl.pallas_call(
        functools.partial(_ret_kernel, steps_per_seq=seq // ROWS, chunk_decay=tuple(cd)),
        grid=(t // ROWS,),
        in_specs=[pl.BlockSpec((ROWS, SEG), lambda i: (i, GATE_COLS // SEG + 2)),
                  pl.BlockSpec((ROWS, RET_DK), lambda i: (i, 0)),
                  pl.BlockSpec((ROWS, RET_DK), lambda i: (i, 0)),
                  pl.BlockSpec((RET_HEADS, CHUNK, CHUNK), lambda i: (0, 0, 0)),
                  pl.BlockSpec((RET_HEADS, CHUNK, RET_DK), lambda i: (0, 0, 0)),
                  pl.BlockSpec((RET_HEADS, CHUNK, RET_DV), lambda i: (0, 0, 0))],
        out_specs=pl.BlockSpec((ROWS, MIX), lambda i: (i, 0)),
        out_shape=jax.ShapeDtypeStruct((t, MIX), BF16),
        scratch_shapes=[pltpu.VMEM((RET_HEADS, RET_DV, RET_DK), F32)],
        compiler_params=_cparams(1),
        name="retention_mixer",
    )(u, cos2, sin2, jnp.asarray(dmask), jnp.asarray(qs_b), jnp.asarray(vs_b))


def _rwkv_kernel(*refs, steps_per_seq, first_layer):
    if first_layer:
        (u_ref, sm_ref, mu_ref, mus_ref, w0_ref, w2_ref, a0_ref, a2_ref, g2_ref, kk_ref, ka_ref,
         rk_ref, lnw_ref, lnb_ref, o_ref, vf_out_ref,
         st_ref, cu_ref, cs_ref, r_s, lw_s, k_s, v_s, kk_s, a_s, g_s) = refs
    else:
        (u_ref, sm_ref, mu_ref, mus_ref, w0_ref, w2_ref, a0_ref, a2_ref, g2_ref, kk_ref, ka_ref,
         rk_ref, lnw_ref, lnb_ref, vf_ref, v0_ref, v1_ref, v2_ref, o_ref,
         st_ref, cu_ref, cs_ref, r_s, lw_s, k_s, v_s, kk_s, a_s, g_s) = refs

    @pl.when(pl.program_id(0) % steps_per_seq == 0)
    def _():
        st_ref[...] = jnp.zeros_like(st_ref)
        cu_ref[...] = jnp.zeros_like(cu_ref)
        cs_ref[...] = jnp.zeros_like(cs_ref)

    row = lax.broadcasted_iota(jnp.int32, (ROWS, 1), 0)

    def shifted(x, carry_ref, mu):
        prev = jnp.where(row == 0, carry_ref[0:1, :], pltpu.roll(x, 1, 0))
        carry_ref[0:1, :] = x[ROWS - 1:ROWS, :]
        return x + (prev - x) * mu

    us = shifted(u_ref[...], cu_ref, mu_ref[...])
    ss = shifted(sm_ref[...], cs_ref, mus_ref[...])
    r = us[:, 0:MIX]
    k = us[:, MIX:2 * MIX]
    v = us[:, 2 * MIX:3 * MIX]
    wl = ss[:, SM_WL:SM_WL + LANES]
    al = ss[:, SM_AL:SM_AL + LANES]
    gl = ss[:, SM_GL:SM_GL + 2 * LANES]
    w = w0_ref[...] + _dot(jnp.tanh(wl), w2_ref[...])
    lw_s[...] = -jnp.exp(-_softplus(-w) - 0.5)
    a = _sigmoid(a0_ref[...] + _dot(al, a2_ref[...]))
    g_s[...] = _dot(_sigmoid(gl), g2_ref[...])
    if first_layer:
        vf_out_ref[...] = v
    else:
        mix = _sigmoid(v0_ref[...] + _dot(_dot(v, v1_ref[...]), v2_ref[...]))
        v = v + (vf_ref[...] - v) * mix
    r_s[...] = r
    v_s[...] = v
    a_s[...] = a
    kk_s[...] = k * kk_ref[...]
    k_s[...] = k * (1.0 + (a - 1.0) * ka_ref[...])

    lane = lax.broadcasted_iota(jnp.int32, (1, LANES), 1)
    mlo = (lane < RWKV_HEAD).astype(F32)
    mhi = 1.0 - mlo
    rr = lax.broadcasted_iota(jnp.int32, (LANES, LANES), 0)
    cc = lax.broadcasted_iota(jnp.int32, (LANES, LANES), 1)
    bd_mask = (rr // RWKV_HEAD) == (cc // RWKV_HEAD)
    bd = bd_mask.astype(BF16)
    eye = (rr == cc).astype(F32)
    tri = _tri_incl(CHUNK).astype(BF16)
    trow = lax.broadcasted_iota(jnp.int32, (CHUNK, LANES), 0)
    scol = lax.broadcasted_iota(jnp.int32, (CHUNK, LANES), 1) % RWKV_HEAD
    strict = scol < trow
    incl = scol <= trow
    inv_n = 1.0 / RWKV_HEAD

    def stack(x):
        return jnp.concatenate([x * mlo, x * mhi], axis=0)

    pairs = range(RWKV_PAIRS)
    cols = [slice(p * LANES, (p + 1) * LANES) for p in pairs]
    n_double = int(math.log2(CHUNK)) - 1

    def chunk(c):
        rows = slice(c * CHUNK, (c + 1) * CHUNK)
        tile = lambda ref: [ref[rows, cols[p]] for p in pairs]
        r_t, lw_t, k_t, v_t, kkr, a_t, g_t = map(tile, (r_s, lw_s, k_s, v_s, kk_s, a_s, g_s))
        nrm = [jnp.sqrt(_dot_exact_rhs(kkr[p] * kkr[p], bd, 2)) for p in pairs]
        cum = [_dot_lhs_exact(tri, lw_t[p], 3) for p in pairs]
        kk = [kkr[p] / jnp.maximum(nrm[p], 1e-12) for p in pairs]
        bv = [kk[p] * a_t[p] for p in pairs]
        cl = [cum[p][CHUNK - 1:CHUNK, :] for p in pairs]
        e_out = [jnp.exp(-cum[p]) for p in pairs]
        e_end = [jnp.exp(cl[p] - cum[p]) for p in pairs]
        at = [-kk[p] * jnp.exp(cum[p] - lw_t[p]) for p in pairs]
        rt = [r_t[p] * jnp.exp(cum[p]) for p in pairs]
        gm = [_dot_nt(jnp.concatenate([stack(at[p]), stack(rt[p])], axis=0),
                      jnp.concatenate([bv[p] * e_out[p], k_t[p] * e_out[p]], axis=0))
              for p in pairs]
        g0 = [jnp.where(strict, gm[p][0:CHUNK], 0.0) for p in pairs]
        g1 = [jnp.where(strict, gm[p][CHUNK:2 * CHUNK], 0.0) for p in pairs]
        m0 = [jnp.where(incl, gm[p][2 * CHUNK:3 * CHUNK], 0.0) for p in pairs]
        m1 = [jnp.where(incl, gm[p][3 * CHUNK:4 * CHUNK], 0.0) for p in pairs]
        swap = lambda x: pltpu.roll(x, RWKV_HEAD, 1)
        ak = [swap(g0[p]) * mlo + g1[p] * mhi for p in pairs]
        rb = [m0[p] * mlo + swap(m1[p]) * mhi for p in pairs]
        rk = [swap(m0[p]) * mlo + m1[p] * mhi for p in pairs]
        npow = [jnp.concatenate([g0[p] * mlo, swap(g1[p]) * mhi], axis=0) for p in pairs]
        tinv = [eye + npow[p] for p in pairs]
        for _ in range(n_double):
            npow = [_dot3(npow[p], npow[p]) for p in pairs]
            tinv = [tinv[p] + _dot3(tinv[p], npow[p]) for p in pairs]
        tcat = [tinv[p][0:CHUNK] + tinv[p][CHUNK:2 * CHUNK] for p in pairs]
        vs2 = [stack(v_t[p]) for p in pairs]
        akv = [_dot(ak[p], vs2[p]) for p in pairs]
        rkv = [_dot(rk[p], vs2[p]) for p in pairs]
        bonus = [_dot_exact_rhs(r_t[p] * k_t[p] * rk_ref[:, cols[p]], bd, 2) * v_t[p] for p in pairs]
        kb_end = [jnp.concatenate([bv[p] * e_end[p], k_t[p] * e_end[p]], axis=0) for p in pairs]
        st = [st_ref[p] for p in pairs]
        u_rhs = [_dot_nt(at[p], st[p]) + akv[p] for p in pairs]
        u_t = [_dot(tcat[p], stack(u_rhs[p])) for p in pairs]
        y = [_dot_nt(rt[p], st[p]) + _dot(rb[p], stack(u_t[p])) + rkv[p] for p in pairs]
        upd = [_dot_tn(jnp.concatenate([u_t[p], v_t[p]], axis=0), kb_end[p]) for p in pairs]
        for p in pairs:
            st_ref[p] = st[p] * jnp.exp(cl[p]) + jnp.where(bd_mask, upd[p], 0.0)
        mean = [_dot_exact_rhs(y[p], bd, 2) * inv_n for p in pairs]
        dev = [y[p] - mean[p] for p in pairs]
        var = [_dot_exact_rhs(dev[p] * dev[p], bd, 2) * inv_n for p in pairs]
        for p in pairs:
            yn = dev[p] * lax.rsqrt(var[p] + RWKV_GN_EPS) * lnw_ref[:, cols[p]] + lnb_ref[:, cols[p]]
            o_ref[rows, cols[p]] = ((yn + bonus[p]) * g_t[p]).astype(o_ref.dtype)

    for c in range(ROWS // CHUNK):
        chunk(c)


def _rwkv(u, small, prm, v_first, seq):
    t = u.shape[0]
    first = v_first is None
    row = lambda a: a.reshape(1, -1)
    vec = lambda n: pl.BlockSpec((1, n), lambda i: (0, 0))
    mat = lambda r, c: pl.BlockSpec((r, c), lambda i: (0, 0))
    args = [u, small, row(prm["mu"]), row(prm["mu_s"]), row(prm["w0"]), prm["w2"], row(prm["a0"]), prm["a2"],
            prm["g2"], row(prm["k_k"]), row(prm["k_a"]), row(prm["r_k"]), row(prm["ln_w"]), row(prm["ln_b"])]
    in_specs = [pl.BlockSpec((ROWS, SEG), lambda i: (i, GATE_COLS // SEG + 1)),
                pl.BlockSpec((ROWS, 4 * LANES), lambda i: (i, 0)),
                vec(SEG), vec(4 * LANES), vec(MIX), mat(LANES, MIX), vec(MIX), mat(LANES, MIX),
                mat(2 * LANES, MIX), vec(MIX), vec(MIX), vec(MIX), vec(MIX), vec(MIX)]
    out_block = pl.BlockSpec((ROWS, MIX), lambda i: (i, 0))
    if first:
        out_specs = [out_block, out_block]
        out_shape = [jax.ShapeDtypeStruct((t, MIX), BF16), jax.ShapeDtypeStruct((t, MIX), F32)]
    else:
        args += [v_first, row(prm["v0"]), prm["v1"], prm["v2"]]
        in_specs += [out_block, vec(MIX), mat(MIX, LANES), mat(LANES, MIX)]
        out_specs = out_block
        out_shape = jax.ShapeDtypeStruct((t, MIX), BF16)
    tok = pltpu.VMEM((ROWS, MIX), F32)
    res = pl.pallas_call(
        functools.partial(_rwkv_kernel, steps_per_seq=seq // ROWS, first_layer=first),
        grid=(t // ROWS,),
        in_specs=in_specs,
        out_specs=out_specs,
        out_shape=out_shape,
        scratch_shapes=[pltpu.VMEM((RWKV_PAIRS, LANES, LANES), F32),
                        pltpu.VMEM((8, SEG), F32), pltpu.VMEM((8, 4 * LANES), F32),
                        tok, tok, tok, tok, tok, tok, tok],
        compiler_params=_cparams(1),
        name="rwkv7_mixer",
    )(*args)
    return (res[0], res[1]) if first else (res, v_first)


def _merge_kernel(x_ref, oa_ref, ob_ref, oc_ref, ga_ref, gb_ref, gc_ref, wa_ref, wb_ref, wc_ref,
                  wo_ref, o_ref):
    @pl.when(pl.program_id(1) == 0)
    def _():
        o_ref[...] = x_ref[...]

    m = (_sigmoid(ga_ref[...]) * jnp.dot(oa_ref[...], wa_ref[...], preferred_element_type=F32)
         + _sigmoid(gb_ref[...]) * jnp.dot(ob_ref[...], wb_ref[...], preferred_element_type=F32)
         + _sigmoid(gc_ref[...]) * jnp.dot(oc_ref[...], wc_ref[...], preferred_element_type=F32))
    o_ref[...] += jnp.dot(m.astype(BF16), wo_ref[...], preferred_element_type=F32)


def _merge_out(x, u, o_a, o_b, o_c, wa, wb, wc, wo):
    t, d = x.shape
    nj = d // TN_MERGE
    full = pl.BlockSpec((TM, d), lambda i, j: (i, 0))
    mixo = pl.BlockSpec((TM, MIX), lambda i, j: (i, 0))
    gate = lambda g: pl.BlockSpec((TM, TN_MERGE), lambda i, j: (i, g * nj + j))
    wbr = pl.BlockSpec((MIX, TN_MERGE), lambda i, j: (0, j))
    return pl.pallas_call(
        _merge_kernel,
        grid=(t // TM, nj),
        in_specs=[full, mixo, mixo, mixo, gate(0), gate(1), gate(2), wbr, wbr, wbr,
                  pl.BlockSpec((TN_MERGE, d), lambda i, j: (j, 0))],
        out_specs=full,
        out_shape=jax.ShapeDtypeStruct((t, d), F32),
        compiler_params=_cparams(2),
        name="merge_out_proj",
    )(x, o_a, o_b, o_c, u, u, u, wa, wb, wc, wo)


def _ffn_kernel(x_ref, g_ref, wz_ref, wv_ref, cw_ref, cb_ref, wd_ref, o_ref, h_ref, carry_ref, *,
                tiles_per_seq):
    i, j = pl.program_id(0), pl.program_id(1)

    @pl.when(j == 0)
    def _():
        x = x_ref[...]
        h_ref[...] = _rms_rows(x, g_ref[...]).astype(BF16)
        o_ref[...] = x

    @pl.when(jnp.logical_and(j == 0, i % tiles_per_seq == 0))
    def _():
        carry_ref[...] = jnp.zeros_like(carry_ref)

    h = h_ref[...]
    z = jnp.dot(h, wz_ref[...], preferred_element_type=F32)
    val = jnp.dot(h, wv_ref[...], preferred_element_type=F32)
    slot = pl.ds(pl.multiple_of(j * 8, 8), 8)
    tail = carry_ref[slot, :]
    carry_ref[slot, :] = z[TM - 8:TM, :]
    row = lax.broadcasted_iota(jnp.int32, (TM, 1), 0)
    z1 = jnp.where(row == 0, tail[7:8, :], pltpu.roll(z, 1, 0))
    z2 = jnp.where(row == 0, tail[6:7, :], jnp.where(row == 1, tail[7:8, :], pltpu.roll(z, 2, 0)))
    cw = cw_ref[...]
    zc = z * cw[2:3, :] + z1 * cw[1:2, :] + z2 * cw[0:1, :] + cb_ref[...]
    gelu = 0.5 * zc * (1.0 + jnp.tanh(0.7978845608028654 * (zc + 0.044715 * (zc * zc * zc))))
    o_ref[...] += jnp.dot((gelu * val).astype(BF16), wd_ref[...], preferred_element_type=F32)


def _ffn(x, gain, wz, wv, cw, cb, wd, seq):
    t, d = x.shape
    nj = FF_PAD // TN_FF
    full = pl.BlockSpec((TM, d), lambda i, j: (i, 0))
    return pl.pallas_call(
        functools.partial(_ffn_kernel, tiles_per_seq=seq // TM),
        grid=(t // TM, nj),
        in_specs=[full, pl.BlockSpec((1, d), lambda i, j: (0, 0)),
                  pl.BlockSpec((d, TN_FF), lambda i, j: (0, j)),
                  pl.BlockSpec((d, TN_FF), lambda i, j: (0, j)),
                  pl.BlockSpec((CONV_W, TN_FF), lambda i, j: (0, j)),
                  pl.BlockSpec((1, TN_FF), lambda i, j: (0, j)),
                  pl.BlockSpec((TN_FF, d), lambda i, j: (j, 0))],
        out_specs=full,
        out_shape=jax.ShapeDtypeStruct((t, d), F32),
        scratch_shapes=[pltpu.VMEM((TM, d), BF16), pltpu.VMEM((nj * 8, TN_FF), F32)],
        compiler_params=_cparams(2),
        name="ffn_geglu_conv",
    )(x, gain.reshape(1, d), wz, wv, cw, cb.reshape(1, FF_PAD), wd)


def _ple_kernel(x_ref, g_ref, wg_ref, p_ref, wp_ref, fg_ref, o_ref, *, final):
    x = x_ref[...]
    h = _rms_rows(x, g_ref[...])
    gate = _sigmoid(_dot(h, wg_ref[...]))
    y = x + gate * _dot(p_ref[...], wp_ref[...])
    if final:
        y = _rms_rows(y, fg_ref[...])
    o_ref[...] = y


def _ple(x, gain, wg, p, wp, final_gain, final):
    t, d = x.shape
    full = pl.BlockSpec((TM, d), lambda i: (i, 0))
    vec = pl.BlockSpec((1, d), lambda i: (0, 0))
    return pl.pallas_call(
        functools.partial(_ple_kernel, final=final),
        grid=(t // TM,),
        in_specs=[full, vec, pl.BlockSpec((d, d), lambda i: (0, 0)),
                  pl.BlockSpec((TM, PLE_DIM), lambda i: (i, 0)),
                  pl.BlockSpec((PLE_DIM, d), lambda i: (0, 0)), vec],
        out_specs=full,
        out_shape=jax.ShapeDtypeStruct((t, d), F32),
        compiler_params=_cparams(1),
        name="ple_embed",
    )(x, gain.reshape(1, d), wg, p, wp, final_gain.reshape(1, d))


def _pad_cols(a, n):
    return jnp.pad(a, ((0, 0), (0, n - a.shape[1])))


def _pad_rows(a, n):
    return jnp.pad(a, ((0, n - a.shape[0]), (0, 0)))


def _pad_vec(a, n):
    return jnp.pad(a, (0, n - a.shape[0]))


def _layout_w_in(w):
    o_rwkv = GLA_COLS
    o_ret = GLA_COLS + RWKV_COLS
    o_gate = o_ret + RET_COLS
    main = jnp.concatenate([w[:, o_gate:o_gate + GATE_COLS], w[:, 0:SEG],
                            w[:, o_rwkv:o_rwkv + SEG], w[:, o_ret:o_ret + SEG]], axis=1)
    o = o_rwkv + SEG
    small = jnp.concatenate([
        _pad_cols(w[:, o:o + RWKV_W_LORA], LANES),
        _pad_cols(w[:, o + RWKV_W_LORA:o + RWKV_W_LORA + RWKV_A_LORA], LANES),
        _pad_cols(w[:, o + RWKV_W_LORA + RWKV_A_LORA:o + RWKV_W_LORA + RWKV_A_LORA + RWKV_G_LORA], 2 * LANES),
        _pad_cols(w[:, SEG:SEG + GLA_LORA], LANES)], axis=1)
    return main.astype(BF16), small.astype(BF16)


def _layout_mu(mu):
    o = SEG
    small = jnp.concatenate([
        _pad_vec(mu[o:o + RWKV_W_LORA], LANES),
        _pad_vec(mu[o + RWKV_W_LORA:o + RWKV_W_LORA + RWKV_A_LORA], LANES),
        _pad_vec(mu[o + RWKV_W_LORA + RWKV_A_LORA:], 2 * LANES)])
    return mu[:SEG], small


def kernel(x, p, positions, mix_norm, w_in, gla_w_decay, gla_b_decay, gla_norm, rwkv_mu, rwkv_w0, rwkv_w2, rwkv_a0, rwkv_a2, rwkv_g2, rwkv_k_k, rwkv_k_a, rwkv_r_k, rwkv_ln_w, rwkv_ln_b, rwkv_v0, rwkv_v1, rwkv_v2, w_br_gla, w_br_rwkv, w_br_ret, w_o, ffn_norm, w_up, conv_w, conv_b, w_down, ple_norm, w_ple_gate, w_ple_proj, final_norm):
    b, s, d = x.shape
    depth = w_in.shape[0]
    t = b * s
    assert d == D_MODEL and s % TM == 0 and s % ROWS == 0
    xf = x.reshape(t, d)
    cos2, sin2 = _rope_tables(positions.astype(F32).reshape(t, 1))
    v_first = None
    for i in range(depth):
        w_main, w_small = _layout_w_in(w_in[i])
        u = _norm_matmul(xf, mix_norm[i], w_main, TN_IN, "in_proj_main")
        small = _norm_matmul(xf, mix_norm[i], w_small, N_SMALL, "in_proj_small")
        o_a = _gla(u, small, _pad_rows(gla_w_decay[i], LANES), gla_b_decay[i], gla_norm[i], s)
        mu_main, mu_small = _layout_mu(rwkv_mu[i])
        prm = dict(mu=mu_main, mu_s=mu_small, w0=rwkv_w0[i], w2=_pad_rows(rwkv_w2[i], LANES).astype(BF16),
                   a0=rwkv_a0[i], a2=_pad_rows(rwkv_a2[i], LANES).astype(BF16),
                   g2=_pad_rows(rwkv_g2[i], 2 * LANES).astype(BF16), k_k=rwkv_k_k[i], k_a=rwkv_k_a[i],
                   r_k=rwkv_r_k[i].reshape(MIX), ln_w=rwkv_ln_w[i], ln_b=rwkv_ln_b[i])
        if i > 0:
            prm.update(v0=rwkv_v0[i - 1], v1=_pad_cols(rwkv_v1[i - 1], LANES).astype(BF16),
                       v2=_pad_rows(rwkv_v2[i - 1], LANES).astype(BF16))
        o_b, v_first = _rwkv(u, small, prm, v_first, s)
        o_c = _retention(u, cos2, sin2, s)
        xf = _merge_out(xf, u, o_a, o_b, o_c, w_br_gla[i].astype(BF16), w_br_rwkv[i].astype(BF16),
                        w_br_ret[i].astype(BF16), w_o[i].astype(BF16))
        wz = _pad_cols(w_up[i][:, :D_FF], FF_PAD).astype(BF16)
        wv = _pad_cols(w_up[i][:, D_FF:], FF_PAD).astype(BF16)
        xf = _ffn(xf, ffn_norm[i], wz, wv, _pad_cols(conv_w[i], FF_PAD), _pad_vec(conv_b[i], FF_PAD),
                  _pad_rows(w_down[i], FF_PAD).astype(BF16), s)
        xf = _ple(xf, ple_norm[i], w_ple_gate[i].astype(BF16), p[i].reshape(t, PLE_DIM),
                  w_ple_proj[i].astype(BF16), final_norm, final=(i == depth - 1))
    return xf.reshape(b, s, d)
```

```python
import functools
import math

import numpy as np
import jax
import jax.numpy as jnp
from jax import lax
from jax.experimental import pallas as pl
from jax.experimental.pallas import tpu as pltpu

F32 = jnp.float32
BF16 = jnp.bfloat16

D_MODEL = 2048
PLE_DIM = 256
MIX = D_MODEL // 2
CHUNK = 64
EPS = 1e-6
GLA_HEADS, GLA_DK, GLA_DV, GLA_LORA = 4, 128, 256, 16
GLA_GATE_NORMALIZER = 16.0
RWKV_HEAD = 64
RWKV_PAIRS = MIX // (2 * RWKV_HEAD)
RWKV_W_LORA, RWKV_A_LORA, RWKV_V_LORA, RWKV_G_LORA = 64, 64, 32, 160
RWKV_GN_EPS = 64e-5
RET_HEADS, RET_DK, RET_DV = 4, 128, 256
ROPE_BASE = 10000.0
D_FF = 5504
CONV_W = 3
GLA_QK = GLA_HEADS * GLA_DK
RET_QK = RET_HEADS * RET_DK
GLA_COLS = 2 * GLA_QK + 2 * MIX + GLA_LORA
RWKV_COLS = 3 * MIX + RWKV_W_LORA + RWKV_A_LORA + RWKV_G_LORA
RET_COLS = 2 * RET_QK + 2 * MIX
GATE_COLS = 3 * D_MODEL

LANES = 128
VMEM_LIMIT = 56 * 1024 * 1024

SEG = 3 * MIX
N_MAIN = GATE_COLS + 3 * SEG
SM_WL, SM_AL, SM_GL, SM_LR = 0, 128, 256, 512
N_SMALL = 640
FF_PAD = 5632

TM = 512
TM_IN = 1024
TN_IN = 1024
TN_MERGE = 512
TN_FF = 512
ROWS = 256
RWKV_GROUP = 8


def _cparams(n_axes):
    return pltpu.CompilerParams(dimension_semantics=("arbitrary",) * n_axes,
                                vmem_limit_bytes=VMEM_LIMIT)


def _dot(a, b):
    return jnp.dot(a.astype(BF16), b.astype(BF16), preferred_element_type=F32)


def _dot_nt(a, b):
    return lax.dot_general(a.astype(BF16), b.astype(BF16), (((1,), (1,)), ((), ())),
                           preferred_element_type=F32)


def _dot_tn(a, b):
    return lax.dot_general(a.astype(BF16), b.astype(BF16), (((0,), (0,)), ((), ())),
                           preferred_element_type=F32)


def _split(x, parts):
    out = []
    for _ in range(parts):
        p = x.astype(BF16)
        out.append(p)
        x = x - p.astype(F32)
    return out


def _dot_exact_rhs(x, m_bf16, parts):
    acc = None
    for p in _split(x, parts):
        t = jnp.dot(p, m_bf16, preferred_element_type=F32)
        acc = t if acc is None else acc + t
    return acc


def _dot_lhs_exact(m_bf16, x, parts):
    acc = None
    for p in _split(x, parts):
        t = jnp.dot(m_bf16, p, preferred_element_type=F32)
        acc = t if acc is None else acc + t
    return acc


def _sigmoid(x):
    return 1.0 / (1.0 + jnp.exp(-x))


def _silu(x):
    return x * _sigmoid(x)


def _softplus(x):
    return jnp.maximum(x, 0.0) + jnp.log1p(jnp.exp(-jnp.abs(x)))


def _rms_rows(x, gain):
    return x * lax.rsqrt(jnp.mean(x * x, axis=-1, keepdims=True) + EPS) * gain


def _tri_incl(n):
    r = lax.broadcasted_iota(jnp.int32, (n, n), 0)
    c = lax.broadcasted_iota(jnp.int32, (n, n), 1)
    return r >= c


def _in_proj_kernel(x_ref, g_ref, w_ref, ws_ref, o_ref, os_ref, h_ref):
    @pl.when(pl.program_id(1) == 0)
    def _():
        h = _rms_rows(x_ref[...], g_ref[...]).astype(BF16)
        h_ref[...] = h
        os_ref[...] = jnp.dot(h, ws_ref[...], preferred_element_type=F32)

    o_ref[...] = jnp.dot(h_ref[...], w_ref[...], preferred_element_type=F32).astype(o_ref.dtype)


def _in_proj(x, gain, w_main, w_small):
    t, d = x.shape
    return pl.pallas_call(
        _in_proj_kernel,
        grid=(t // TM_IN, N_MAIN // TN_IN),
        in_specs=[pl.BlockSpec((TM_IN, d), lambda i, j: (i, 0)),
                  pl.BlockSpec((1, d), lambda i, j: (0, 0)),
                  pl.BlockSpec((d, TN_IN), lambda i, j: (0, j)),
                  pl.BlockSpec((d, N_SMALL), lambda i, j: (0, 0))],
        out_specs=[pl.BlockSpec((TM_IN, TN_IN), lambda i, j: (i, j)),
                   pl.BlockSpec((TM_IN, N_SMALL), lambda i, j: (i, 0))],
        out_shape=[jax.ShapeDtypeStruct((t, N_MAIN), BF16), jax.ShapeDtypeStruct((t, N_SMALL), F32)],
        scratch_shapes=[pltpu.VMEM((TM_IN, d), BF16)],
        compiler_params=_cparams(2),
        name="in_proj",
    )(x, gain.reshape(1, d), w_main, w_small)


def _rope_kernel(pos_ref, f_ref, s_ref, cos_ref, sin_ref):
    ang = pos_ref[...] * f_ref[...]
    cos_ref[...] = jnp.cos(ang)
    sin_ref[...] = jnp.sin(ang) * s_ref[...]


def _rope_tables(pos):
    t = pos.shape[0]
    half = RET_DK // 2
    inv = (ROPE_BASE ** (-np.arange(0, RET_DK, 2, dtype=np.float32) / np.float32(RET_DK))).astype(np.float32)
    inv2 = jnp.asarray(np.concatenate([inv, inv]).reshape(1, RET_DK))
    sign = jnp.asarray(np.concatenate([-np.ones(half, np.float32), np.ones(half, np.float32)]).reshape(1, RET_DK))
    rows = TM
    return pl.pallas_call(
        _rope_kernel,
        grid=(t // rows,),
        in_specs=[pl.BlockSpec((rows, 1), lambda i: (i, 0)),
                  pl.BlockSpec((1, RET_DK), lambda i: (0, 0)),
                  pl.BlockSpec((1, RET_DK), lambda i: (0, 0))],
        out_specs=[pl.BlockSpec((rows, RET_DK), lambda i: (i, 0))] * 2,
        out_shape=[jax.ShapeDtypeStruct((t, RET_DK), F32)] * 2,
        compiler_params=_cparams(1),
        name="rope_tables",
    )(pos, inv2, sign)


def _gla_kernel(u_ref, lr_ref, wd_ref, bd_ref, ng_ref, o_ref, st_ref, la_ref, *, steps_per_seq):
    @pl.when(pl.program_id(0) % steps_per_seq == 0)
    def _():
        st_ref[...] = jnp.zeros_like(st_ref)

    z = jnp.dot(lr_ref[...], wd_ref[...], precision=lax.Precision.HIGHEST,
                preferred_element_type=F32) + bd_ref[...]
    la_ref[...] = -_softplus(-z) * (1.0 / GLA_GATE_NORMALIZER)

    causal = _tri_incl(CHUNK)
    tri = causal.astype(BF16)
    scale = GLA_DK ** -0.5
    ng = ng_ref[...]

    heads = range(GLA_HEADS)
    kcol = [slice(h * GLA_DK, (h + 1) * GLA_DK) for h in heads]
    vcol = [slice(h * GLA_DV, (h + 1) * GLA_DV) for h in heads]

    def chunk(c):
        rows = slice(c * CHUNK, (c + 1) * CHUNK)
        seg = lambda off, col: [u_ref[rows, off + col[h].start:off + col[h].stop].astype(F32) for h in heads]
        q, k = seg(0, kcol), seg(GLA_QK, kcol)
        v, g = seg(2 * GLA_QK, vcol), seg(2 * GLA_QK + MIX, vcol)
        cum = [_dot_lhs_exact(tri, la_ref[rows, kcol[h]], 2) for h in heads]
        cl = [cum[h][CHUNK - 1:CHUNK, :] for h in heads]
        q_in = [(q[h] * jnp.exp(cum[h]) * scale).astype(BF16) for h in heads]
        k_in = [k[h] * jnp.exp(-cum[h]) for h in heads]
        k_st = [k[h] * jnp.exp(cl[h] - cum[h]) for h in heads]
        vb = [v[h].astype(BF16) for h in heads]
        sc = [jnp.where(causal, _dot_nt(q_in[h], k_in[h]), 0.0) for h in heads]
        st = [st_ref[h] for h in heads]
        o = [_dot(sc[h], vb[h]) + _dot_nt(q_in[h], st[h]) for h in heads]
        upd = [_dot_tn(vb[h], k_st[h]) for h in heads]
        for h in heads:
            st_ref[h] = st[h] * jnp.exp(cl[h]) + upd[h]
        for h in heads:
            y = o[h] * lax.rsqrt(jnp.mean(o[h] * o[h], axis=-1, keepdims=True) + EPS) * ng
            o_ref[rows, vcol[h]] = (y * _silu(g[h])).astype(o_ref.dtype)

    for c in range(ROWS // CHUNK):
        chunk(c)


def _gla(u, small, wd_pad, b_decay, norm_g, seq):
    t = u.shape[0]
    return pl.pallas_call(
        functools.partial(_gla_kernel, steps_per_seq=seq // ROWS),
        grid=(t // ROWS,),
        in_specs=[pl.BlockSpec((ROWS, SEG), lambda i: (i, GATE_COLS // SEG)),
                  pl.BlockSpec((ROWS, LANES), lambda i: (i, SM_LR // LANES)),
                  pl.BlockSpec((LANES, GLA_QK), lambda i: (0, 0)),
                  pl.BlockSpec((1, GLA_QK), lambda i: (0, 0)),
                  pl.BlockSpec((1, GLA_DV), lambda i: (0, 0))],
        out_specs=pl.BlockSpec((ROWS, MIX), lambda i: (i, 0)),
        out_shape=jax.ShapeDtypeStruct((t, MIX), BF16),
        scratch_shapes=[pltpu.VMEM((GLA_HEADS, GLA_DV, GLA_DK), F32),
                        pltpu.VMEM((ROWS, GLA_QK), F32)],
        compiler_params=_cparams(1),
        name="gla_mixer",
    )(u, small, wd_pad, b_decay.reshape(1, GLA_QK), norm_g.reshape(1, GLA_DV))


def _ret_consts():
    lg = np.log(1.0 - np.exp2(-5.0 - np.arange(RET_HEADS, dtype=np.float32))).astype(np.float32)
    pos = np.arange(CHUNK, dtype=np.float32)
    diff = pos[:, None] - pos[None, :]
    causal = diff >= 0
    dmask = np.where(causal, np.exp(lg[:, None, None] * np.where(causal, diff, 0.0)), 0.0).astype(np.float32)
    qs = np.exp(lg[:, None] * (pos[None, :] + 1.0)).astype(np.float32)
    vs = np.exp(lg[:, None] * (CHUNK - 1.0 - pos[None, :])).astype(np.float32)
    cd = [float(np.exp(np.float32(l) * np.float32(CHUNK))) for l in lg]
    qs_b = np.broadcast_to(qs[:, :, None], (RET_HEADS, CHUNK, RET_DK)).copy()
    vs_b = np.broadcast_to(vs[:, :, None], (RET_HEADS, CHUNK, RET_DV)).copy()
    return dmask, qs_b, vs_b, cd


def _ret_kernel(u_ref, cos_ref, sin_ref, dm_ref, qs_ref, vs_ref, o_ref, st_ref, *, steps_per_seq, chunk_decay):
    @pl.when(pl.program_id(0) % steps_per_seq == 0)
    def _():
        st_ref[...] = jnp.zeros_like(st_ref)

    scale = RET_DK ** -0.5
    half = RET_DK // 2

    heads = range(RET_HEADS)
    kcol = [slice(h * RET_DK, (h + 1) * RET_DK) for h in heads]
    vcol = [slice(h * RET_DV, (h + 1) * RET_DV) for h in heads]

    def chunk(c):
        rows = slice(c * CHUNK, (c + 1) * CHUNK)
        cos = cos_ref[rows, :]
        sin = sin_ref[rows, :]
        seg = lambda off, col: [u_ref[rows, off + col[h].start:off + col[h].stop].astype(F32) for h in heads]
        q, k = seg(0, kcol), seg(RET_QK, kcol)
        v, g = seg(2 * RET_QK, vcol), seg(2 * RET_QK + MIX, vcol)
        q = [(q[h] * cos + pltpu.roll(q[h], half, 1) * sin) * scale for h in heads]
        k = [(k[h] * cos + pltpu.roll(k[h], half, 1) * sin).astype(BF16) for h in heads]
        sc = [_dot_nt(q[h], k[h]) * dm_ref[h] for h in heads]
        st = [st_ref[h] for h in heads]
        o = [_dot(sc[h], v[h]) + _dot_nt(q[h] * qs_ref[h], st[h]) for h in heads]
        upd = [_dot_tn(v[h] * vs_ref[h], k[h]) for h in heads]
        for h in heads:
            st_ref[h] = st[h] * chunk_decay[h] + upd[h]
        for h in heads:
            y = o[h] * lax.rsqrt(jnp.mean(o[h] * o[h], axis=-1, keepdims=True) + EPS)
            o_ref[rows, vcol[h]] = (_silu(g[h]) * y).astype(o_ref.dtype)

    for c in range(ROWS // CHUNK):
        chunk(c)


def _retention(u, cos2, sin2, seq):
    t = u.shape[0]
    dmask, qs_b, vs_b, cd = _ret_consts()
    return pl.pallas_call(
        functools.partial(_ret_kernel, steps_per_seq=seq // ROWS, chunk_decay=tuple(cd)),
        grid=(t // ROWS,),
        in_specs=[pl.BlockSpec((ROWS, SEG), lambda i: (i, GATE_COLS // SEG + 2)),
                  pl.BlockSpec((ROWS, RET_DK), lambda i: (i, 0)),
                  pl.BlockSpec((ROWS, RET_DK), lambda i: (i, 0)),
                  pl.BlockSpec((RET_HEADS, CHUNK, CHUNK), lambda i: (0, 0, 0)),
                  pl.BlockSpec((RET_HEADS, CHUNK, RET_DK), lambda i: (0, 0, 0)),
                  pl.BlockSpec((RET_HEADS, CHUNK, RET_DV), lambda i: (0, 0, 0))],
        out_specs=pl.BlockSpec((ROWS, MIX), lambda i: (i, 0)),
        out_shape=jax.ShapeDtypeStruct((t, MIX), BF16),
        scratch_shapes=[pltpu.VMEM((RET_HEADS, RET_DV, RET_DK), F32)],
        compiler_params=_cparams(1),
        name="retention_mixer",
    )(u, cos2, sin2, jnp.asarray(dmask), jnp.asarray(qs_b), jnp.asarray(vs_b))


def _rwkv_kernel(*refs, steps_per_seq, first_layer):
    if first_layer:
        (u_ref, sm_ref, mu_ref, mus_ref, w0_ref, w2_ref, a0_ref, a2_ref, g2_ref, kk_ref, ka_ref,
         rk_ref, lnw_ref, lnb_ref, o_ref, vf_out_ref,
         st_ref, cu_ref, cs_ref, r_s, lw_s, k_s, v_s, kk_s, a_s, g_s) = refs
    else:
        (u_ref, sm_ref, mu_ref, mus_ref, w0_ref, w2_ref, a0_ref, a2_ref, g2_ref, kk_ref, ka_ref,
         rk_ref, lnw_ref, lnb_ref, vf_ref, v0_ref, v1_ref, v2_ref, o_ref,
         st_ref, cu_ref, cs_ref, r_s, lw_s, k_s, v_s, kk_s, a_s, g_s) = refs

    @pl.when(pl.program_id(0) % steps_per_seq == 0)
    def _():
        st_ref[...] = jnp.zeros_like(st_ref)
        cu_ref[...] = jnp.zeros_like(cu_ref)
        cs_ref[...] = jnp.zeros_like(cs_ref)

    row = lax.broadcasted_iota(jnp.int32, (ROWS, 1), 0)

    def shifted(x, carry_ref, mu):
        prev = jnp.where(row == 0, carry_ref[0:1, :], pltpu.roll(x, 1, 0))
        carry_ref[0:1, :] = x[ROWS - 1:ROWS, :]
        return x + (prev - x) * mu

    us = shifted(u_ref[...].astype(F32), cu_ref, mu_ref[...])
    ss = shifted(sm_ref[...], cs_ref, mus_ref[...])
    r = us[:, 0:MIX]
    k = us[:, MIX:2 * MIX]
    v = us[:, 2 * MIX:3 * MIX]
    wl = ss[:, SM_WL:SM_WL + LANES]
    al = ss[:, SM_AL:SM_AL + LANES]
    gl = ss[:, SM_GL:SM_GL + 2 * LANES]
    w = w0_ref[...] + _dot(jnp.tanh(wl), w2_ref[...])
    lw_s[...] = -jnp.exp(-_softplus(-w) - 0.5)
    a = _sigmoid(a0_ref[...] + _dot(al, a2_ref[...]))
    g_s[...] = _dot(_sigmoid(gl), g2_ref[...])
    if first_layer:
        vf_out_ref[...] = v
    else:
        mix = _sigmoid(v0_ref[...] + _dot(_dot(v, v1_ref[...]), v2_ref[...]))
        v = v + (vf_ref[...] - v) * mix
    r_s[...] = r
    v_s[...] = v
    a_s[...] = a
    kk_s[...] = k * kk_ref[...]
    k_s[...] = k * (1.0 + (a - 1.0) * ka_ref[...])

    lane = lax.broadcasted_iota(jnp.int32, (1, LANES), 1)
    mlo = (lane < RWKV_HEAD).astype(F32)
    mhi = 1.0 - mlo
    rr = lax.broadcasted_iota(jnp.int32, (LANES, LANES), 0)
    cc = lax.broadcasted_iota(jnp.int32, (LANES, LANES), 1)
    bd_mask = (rr // RWKV_HEAD) == (cc // RWKV_HEAD)
    bd = bd_mask.astype(BF16)
    eye = (rr == cc).astype(F32)
    tri = _tri_incl(CHUNK).astype(BF16)
    trow = lax.broadcasted_iota(jnp.int32, (CHUNK, LANES), 0)
    scol = lax.broadcasted_iota(jnp.int32, (CHUNK, LANES), 1) % RWKV_HEAD
    strict = scol < trow
    incl = scol <= trow
    inv_n = 1.0 / RWKV_HEAD

    def stack(x):
        return jnp.concatenate([x * mlo, x * mhi], axis=0)

    pairs = range(RWKV_GROUP)
    n_double = int(math.log2(CHUNK)) - 1

    def chunk(c, first):
        rows = slice(c * CHUNK, (c + 1) * CHUNK)
        cols = [slice((first + p) * LANES, (first + p + 1) * LANES) for p in pairs]
        tile = lambda ref: [ref[rows, cols[p]] for p in pairs]
        r_t, lw_t, k_t, v_t, kkr, a_t, g_t = map(tile, (r_s, lw_s, k_s, v_s, kk_s, a_s, g_s))
        nrm = [jnp.sqrt(_dot_exact_rhs(kkr[p] * kkr[p], bd, 1)) for p in pairs]
        cum = [_dot_lhs_exact(tri, lw_t[p], 2) for p in pairs]
        kk = [kkr[p] / jnp.maximum(nrm[p], 1e-12) for p in pairs]
        bv = [kk[p] * a_t[p] for p in pairs]
        cl = [cum[p][CHUNK - 1:CHUNK, :] for p in pairs]
        e_out = [jnp.exp(-cum[p]) for p in pairs]
        e_end = [jnp.exp(cl[p] - cum[p]) for p in pairs]
        at = [-kk[p] * jnp.exp(cum[p] - lw_t[p]) for p in pairs]
        rt = [r_t[p] * jnp.exp(cum[p]) for p in pairs]
        gm = [_dot_nt(jnp.concatenate([stack(at[p]), stack(rt[p])], axis=0),
                      jnp.concatenate([bv[p] * e_out[p], k_t[p] * e_out[p]], axis=0))
              for p in pairs]
        g0 = [jnp.where(strict, gm[p][0:CHUNK], 0.0) for p in pairs]
        g1 = [jnp.where(strict, gm[p][CHUNK:2 * CHUNK], 0.0) for p in pairs]
        m0 = [jnp.where(incl, gm[p][2 * CHUNK:3 * CHUNK], 0.0) for p in pairs]
        m1 = [jnp.where(incl, gm[p][3 * CHUNK:4 * CHUNK], 0.0) for p in pairs]
        swap = lambda x: pltpu.roll(x, RWKV_HEAD, 1)
        ak = [swap(g0[p]) * mlo + g1[p] * mhi for p in pairs]
        rb = [m0[p] * mlo + swap(m1[p]) * mhi for p in pairs]
        rk = [swap(m0[p]) * mlo + m1[p] * mhi for p in pairs]
        npow = [jnp.concatenate([g0[p] * mlo, swap(g1[p]) * mhi], axis=0) for p in pairs]
        tinv = [eye + npow[p] for p in pairs]
        for _ in range(n_double):
            npow = [_dot(npow[p], npow[p]) for p in pairs]
            tinv = [tinv[p] + _dot(tinv[p], npow[p]) for p in pairs]
        tcat = [tinv[p][0:CHUNK] + tinv[p][CHUNK:2 * CHUNK] for p in pairs]
        vs2 = [stack(v_t[p]) for p in pairs]
        akv = [_dot(ak[p], vs2[p]) for p in pairs]
        rkv = [_dot(rk[p], vs2[p]) for p in pairs]
        bonus = [_dot_exact_rhs(r_t[p] * k_t[p] * rk_ref[:, cols[p]], bd, 1) * v_t[p] for p in pairs]
        kb_end = [jnp.concatenate([bv[p] * e_end[p], k_t[p] * e_end[p]], axis=0) for p in pairs]
        st = [st_ref[first + p] for p in pairs]
        u_rhs = [_dot_nt(at[p], st[p]) + akv[p] for p in pairs]
        u_t = [_dot(tcat[p], stack(u_rhs[p])) for p in pairs]
        y = [_dot_nt(rt[p], st[p]) + _dot(rb[p], stack(u_t[p])) + rkv[p] for p in pairs]
        upd = [_dot_tn(jnp.concatenate([u_t[p], v_t[p]], axis=0), kb_end[p]) for p in pairs]
        for p in pairs:
            st_ref[first + p] = st[p] * jnp.exp(cl[p]) + jnp.where(bd_mask, upd[p], 0.0)
        mean = [_dot_exact_rhs(y[p], bd, 1) * inv_n for p in pairs]
        dev = [y[p] - mean[p] for p in pairs]
        var = [_dot_exact_rhs(dev[p] * dev[p], bd, 1) * inv_n for p in pairs]
        for p in pairs:
            yn = dev[p] * lax.rsqrt(var[p] + RWKV_GN_EPS) * lnw_ref[:, cols[p]] + lnb_ref[:, cols[p]]
            o_ref[rows, cols[p]] = ((yn + bonus[p]) * g_t[p]).astype(o_ref.dtype)

    for c in range(ROWS // CHUNK):
        for first in range(0, RWKV_PAIRS, RWKV_GROUP):
            chunk(c, first)


def _rwkv(u, small, prm, v_first, seq):
    t = u.shape[0]
    first = v_first is None
    row = lambda a: a.reshape(1, -1)
    vec = lambda n: pl.BlockSpec((1, n), lambda i: (0, 0))
    mat = lambda r, c: pl.BlockSpec((r, c), lambda i: (0, 0))
    args = [u, small, row(prm["mu"]), row(prm["mu_s"]), row(prm["w0"]), prm["w2"], row(prm["a0"]), prm["a2"],
            prm["g2"], row(prm["k_k"]), row(prm["k_a"]), row(prm["r_k"]), row(prm["ln_w"]), row(prm["ln_b"])]
    in_specs = [pl.BlockSpec((ROWS, SEG), lambda i: (i, GATE_COLS // SEG + 1)),
                pl.BlockSpec((ROWS, 4 * LANES), lambda i: (i, 0)),
                vec(SEG), vec(4 * LANES), vec(MIX), mat(LANES, MIX), vec(MIX), mat(LANES, MIX),
                mat(2 * LANES, MIX), vec(MIX), vec(MIX), vec(MIX), vec(MIX), vec(MIX)]
    out_block = pl.BlockSpec((ROWS, MIX), lambda i: (i, 0))
    if first:
        out_specs = [out_block, out_block]
        out_shape = [jax.ShapeDtypeStruct((t, MIX), BF16), jax.ShapeDtypeStruct((t, MIX), F32)]
    else:
        args += [v_first, row(prm["v0"]), prm["v1"], prm["v2"]]
        in_specs += [out_block, vec(MIX), mat(MIX, LANES), mat(LANES, MIX)]
        out_specs = out_block
        out_shape = jax.ShapeDtypeStruct((t, MIX), BF16)
    tok = pltpu.VMEM((ROWS, MIX), F32)
    res = pl.pallas_call(
        functools.partial(_rwkv_kernel, steps_per_seq=seq // ROWS, first_layer=first),
        grid=(t // ROWS,),
        in_specs=in_specs,
        out_specs=out_specs,
        out_shape=out_shape,
        scratch_shapes=[pltpu.VMEM((RWKV_PAIRS, LANES, LANES), F32),
                        pltpu.VMEM((8, SEG), F32), pltpu.VMEM((8, 4 * LANES), F32),
                        tok, tok, tok, tok, tok, tok, tok],
        compiler_params=_cparams(1),
        name="rwkv7_mixer",
    )(*args)
    return (res[0], res[1]) if first else (res, v_first)


def _merge_kernel(x_ref, oa_ref, ob_ref, oc_ref, ga_ref, gb_ref, gc_ref, wa_ref, wb_ref, wc_ref,
                  wo_ref, o_ref):
    @pl.when(pl.program_id(1) == 0)
    def _():
        o_ref[...] = x_ref[...]

    gate = lambda ref: _sigmoid(ref[...].astype(F32))
    m = (gate(ga_ref) * jnp.dot(oa_ref[...], wa_ref[...], preferred_element_type=F32)
         + gate(gb_ref) * jnp.dot(ob_ref[...], wb_ref[...], preferred_element_type=F32)
         + gate(gc_ref) * jnp.dot(oc_ref[...], wc_ref[...], preferred_element_type=F32))
    o_ref[...] += jnp.dot(m.astype(BF16), wo_ref[...], preferred_element_type=F32)


def _merge_out(x, u, o_a, o_b, o_c, wa, wb, wc, wo):
    t, d = x.shape
    nj = d // TN_MERGE
    full = pl.BlockSpec((TM, d), lambda i, j: (i, 0))
    mixo = pl.BlockSpec((TM, MIX), lambda i, j: (i, 0))
    gate = lambda g: pl.BlockSpec((TM, TN_MERGE), lambda i, j: (i, g * nj + j))
    wbr = pl.BlockSpec((MIX, TN_MERGE), lambda i, j: (0, j))
    return pl.pallas_call(
        _merge_kernel,
        grid=(t // TM, nj),
        in_specs=[full, mixo, mixo, mixo, gate(0), gate(1), gate(2), wbr, wbr, wbr,
                  pl.BlockSpec((TN_MERGE, d), lambda i, j: (j, 0))],
        out_specs=full,
        out_shape=jax.ShapeDtypeStruct((t, d), F32),
        compiler_params=_cparams(2),
        name="merge_out_proj",
    )(x, o_a, o_b, o_c, u, u, u, wa, wb, wc, wo)


def _ffn_kernel(x_ref, g_ref, wz_ref, wv_ref, cw_ref, cb_ref, wd_ref, o_ref, h_ref, carry_ref, *,
                tiles_per_seq):
    i, j = pl.program_id(0), pl.program_id(1)

    @pl.when(j == 0)
    def _():
        x = x_ref[...]
        h_ref[...] = _rms_rows(x, g_ref[...]).astype(BF16)
        o_ref[...] = x

    @pl.when(jnp.logical_and(j == 0, i % tiles_per_seq == 0))
    def _():
        carry_ref[...] = jnp.zeros_like(carry_ref)

    h = h_ref[...]
    z = jnp.dot(h, wz_ref[...], preferred_element_type=F32)
    val = jnp.dot(h, wv_ref[...], preferred_element_type=F32)
    slot = pl.ds(pl.multiple_of(j * 8, 8), 8)
    tail = carry_ref[slot, :]
    carry_ref[slot, :] = z[TM - 8:TM, :]
    row = lax.broadcasted_iota(jnp.int32, (TM, 1), 0)
    z1 = jnp.where(row == 0, tail[7:8, :], pltpu.roll(z, 1, 0))
    z2 = jnp.where(row == 0, tail[6:7, :], jnp.where(row == 1, tail[7:8, :], pltpu.roll(z, 2, 0)))
    cw = cw_ref[...]
    zc = z * cw[2:3, :] + z1 * cw[1:2, :] + z2 * cw[0:1, :] + cb_ref[...]
    gelu = 0.5 * zc * (1.0 + jnp.tanh(0.7978845608028654 * (zc + 0.044715 * (zc * zc * zc))))
    o_ref[...] += jnp.dot((gelu * val).astype(BF16), wd_ref[...], preferred_element_type=F32)


def _ffn(x, gain, wz, wv, cw, cb, wd, seq):
    t, d = x.shape
    nj = FF_PAD // TN_FF
    full = pl.BlockSpec((TM, d), lambda i, j: (i, 0))
    return pl.pallas_call(
        functools.partial(_ffn_kernel, tiles_per_seq=seq // TM),
        grid=(t // TM, nj),
        in_specs=[full, pl.BlockSpec((1, d), lambda i, j: (0, 0)),
                  pl.BlockSpec((d, TN_FF), lambda i, j: (0, j)),
                  pl.BlockSpec((d, TN_FF), lambda i, j: (0, j)),
                  pl.BlockSpec((CONV_W, TN_FF), lambda i, j: (0, j)),
                  pl.BlockSpec((1, TN_FF), lambda i, j: (0, j)),
                  pl.BlockSpec((TN_FF, d), lambda i, j: (j, 0))],
        out_specs=full,
        out_shape=jax.ShapeDtypeStruct((t, d), F32),
        scratch_shapes=[pltpu.VMEM((TM, d), BF16), pltpu.VMEM((nj * 8, TN_FF), F32)],
        compiler_params=_cparams(2),
        name="ffn_geglu_conv",
    )(x, gain.reshape(1, d), wz, wv, cw, cb.reshape(1, FF_PAD), wd)


def _ple_kernel(x_ref, g_ref, wg_ref, p_ref, wp_ref, fg_ref, o_ref, *, final):
    x = x_ref[...]
    h = _rms_rows(x, g_ref[...])
    gate = _sigmoid(_dot(h, wg_ref[...]))
    y = x + gate * _dot(p_ref[...], wp_ref[...])
    if final:
        y = _rms_rows(y, fg_ref[...])
    o_ref[...] = y


def _ple(x, gain, wg, p, wp, final_gain, final):
    t, d = x.shape
    full = pl.BlockSpec((TM, d), lambda i: (i, 0))
    vec = pl.BlockSpec((1, d), lambda i: (0, 0))
    return pl.pallas_call(
        functools.partial(_ple_kernel, final=final),
        grid=(t // TM,),
        in_specs=[full, vec, pl.BlockSpec((d, d), lambda i: (0, 0)),
                  pl.BlockSpec((TM, PLE_DIM), lambda i: (i, 0)),
                  pl.BlockSpec((PLE_DIM, d), lambda i: (0, 0)), vec],
        out_specs=full,
        out_shape=jax.ShapeDtypeStruct((t, d), F32),
        compiler_params=_cparams(1),
        name="ple_embed",
    )(x, gain.reshape(1, d), wg, p, wp, final_gain.reshape(1, d))


def _pad_cols(a, n):
    return jnp.pad(a, ((0, 0), (0, n - a.shape[1])))


def _pad_rows(a, n):
    return jnp.pad(a, ((0, n - a.shape[0]), (0, 0)))


def _pad_vec(a, n):
    return jnp.pad(a, (0, n - a.shape[0]))


def _layout_w_in(w):
    o_rwkv = GLA_COLS
    o_ret = GLA_COLS + RWKV_COLS
    o_gate = o_ret + RET_COLS
    main = jnp.concatenate([w[:, o_gate:o_gate + GATE_COLS], w[:, 0:SEG],
                            w[:, o_rwkv:o_rwkv + SEG], w[:, o_ret:o_ret + SEG]], axis=1)
    o = o_rwkv + SEG
    small = jnp.concatenate([
        _pad_cols(w[:, o:o + RWKV_W_LORA], LANES),
        _pad_cols(w[:, o + RWKV_W_LORA:o + RWKV_W_LORA + RWKV_A_LORA], LANES),
        _pad_cols(w[:, o + RWKV_W_LORA + RWKV_A_LORA:o + RWKV_W_LORA + RWKV_A_LORA + RWKV_G_LORA], 2 * LANES),
        _pad_cols(w[:, SEG:SEG + GLA_LORA], LANES)], axis=1)
    return main.astype(BF16), small.astype(BF16)


def _layout_mu(mu):
    o = SEG
    small = jnp.concatenate([
        _pad_vec(mu[o:o + RWKV_W_LORA], LANES),
        _pad_vec(mu[o + RWKV_W_LORA:o + RWKV_W_LORA + RWKV_A_LORA], LANES),
        _pad_vec(mu[o + RWKV_W_LORA + RWKV_A_LORA:], 2 * LANES)])
    return mu[:SEG], small


def kernel(x, p, positions, mix_norm, w_in, gla_w_decay, gla_b_decay, gla_norm, rwkv_mu, rwkv_w0, rwkv_w2, rwkv_a0, rwkv_a2, rwkv_g2, rwkv_k_k, rwkv_k_a, rwkv_r_k, rwkv_ln_w, rwkv_ln_b, rwkv_v0, rwkv_v1, rwkv_v2, w_br_gla, w_br_rwkv, w_br_ret, w_o, ffn_norm, w_up, conv_w, conv_b, w_down, ple_norm, w_ple_gate, w_ple_proj, final_norm):
    b, s, d = x.shape
    depth = w_in.shape[0]
    t = b * s
    assert d == D_MODEL and s % TM == 0 and s % ROWS == 0 and t % TM_IN == 0
    xf = x.reshape(t, d)
    cos2, sin2 = _rope_tables(positions.astype(F32).reshape(t, 1))
    v_first = None
    for i in range(depth):
        w_main, w_small = _layout_w_in(w_in[i])
        u, small = _in_proj(xf, mix_norm[i], w_main, w_small)
        o_a = _gla(u, small, _pad_rows(gla_w_decay[i], LANES), gla_b_decay[i], gla_norm[i], s)
        mu_main, mu_small = _layout_mu(rwkv_mu[i])
        prm = dict(mu=mu_main, mu_s=mu_small, w0=rwkv_w0[i], w2=_pad_rows(rwkv_w2[i], LANES).astype(BF16),
                   a0=rwkv_a0[i], a2=_pad_rows(rwkv_a2[i], LANES).astype(BF16),
                   g2=_pad_rows(rwkv_g2[i], 2 * LANES).astype(BF16), k_k=rwkv_k_k[i], k_a=rwkv_k_a[i],
                   r_k=rwkv_r_k[i].reshape(MIX), ln_w=rwkv_ln_w[i], ln_b=rwkv_ln_b[i])
        if i > 0:
            prm.update(v0=rwkv_v0[i - 1], v1=_pad_cols(rwkv_v1[i - 1], LANES).astype(BF16),
                       v2=_pad_rows(rwkv_v2[i - 1], LANES).astype(BF16))
        o_b, v_first = _rwkv(u, small, prm, v_first, s)
        o_c = _retention(u, cos2, sin2, s)
        xf = _merge_out(xf, u, o_a, o_b, o_c, w_br_gla[i].astype(BF16), w_br_rwkv[i].astype(BF16),
                        w_br_ret[i].astype(BF16), w_o[i].astype(BF16))
        wz = _pad_cols(w_up[i][:, :D_FF], FF_PAD).astype(BF16)
        wv = _pad_cols(w_up[i][:, D_FF:], FF_PAD).astype(BF16)
        xf = _ffn(xf, ffn_norm[i], wz, wv, _pad_cols(conv_w[i], FF_PAD), _pad_vec(conv_b[i], FF_PAD),
                  _pad_rows(w_down[i], FF_PAD).astype(BF16), s)
        xf = _ple(xf, ple_norm[i], w_ple_gate[i].astype(BF16), p[i].reshape(t, PLE_DIM),
                  w_ple_proj[i].astype(BF16), final_norm, final=(i == depth - 1))
    return xf.reshape(b, s, d)
```

```python
import functools
import math

import numpy as np
import jax
import jax.numpy as jnp
from jax import lax
from jax.experimental import pallas as pl
from jax.experimental.pallas import tpu as pltpu

F32 = jnp.float32
BF16 = jnp.bfloat16

D_MODEL = 2048
PLE_DIM = 256
MIX = D_MODEL // 2
CHUNK = 64
EPS = 1e-6
GLA_HEADS, GLA_DK, GLA_DV, GLA_LORA = 4, 128, 256, 16
GLA_GATE_NORMALIZER = 16.0
RWKV_HEAD = 64
RWKV_PAIRS = MIX // (2 * RWKV_HEAD)
RWKV_W_LORA, RWKV_A_LORA, RWKV_V_LORA, RWKV_G_LORA = 64, 64, 32, 160
RWKV_GN_EPS = 64e-5
RET_HEADS, RET_DK, RET_DV = 4, 128, 256
ROPE_BASE = 10000.0
D_FF = 5504
CONV_W = 3
GELU_CUBIC = 0.044715
GLA_QK = GLA_HEADS * GLA_DK
RET_QK = RET_HEADS * RET_DK
GLA_COLS = 2 * GLA_QK + 2 * MIX + GLA_LORA
RWKV_COLS = 3 * MIX + RWKV_W_LORA + RWKV_A_LORA + RWKV_G_LORA
RET_COLS = 2 * RET_QK + 2 * MIX
GATE_COLS = 3 * D_MODEL

LANES = 128
VMEM_LIMIT = 56 * 1024 * 1024

SEG = 3 * MIX
N_MAIN = GATE_COLS + 3 * SEG
SM_WL, SM_AL, SM_GL, SM_LR = 0, 128, 256, 512
N_SMALL = 640
FF_PAD = 5632

TM = 512
TM_IN = 1024
TN_IN = 1024
TM_MERGE = 256
TN_FF = 512
ROWS = 256


def _cparams(n_axes):
    return pltpu.CompilerParams(dimension_semantics=("arbitrary",) * n_axes,
                                vmem_limit_bytes=VMEM_LIMIT)


def _dot(a, b):
    return jnp.dot(a.astype(BF16), b.astype(BF16), preferred_element_type=F32)


def _dot_nt(a, b):
    return lax.dot_general(a.astype(BF16), b.astype(BF16), (((1,), (1,)), ((), ())),
                           preferred_element_type=F32)


def _dot_tn(a, b):
    return lax.dot_general(a.astype(BF16), b.astype(BF16), (((0,), (0,)), ((), ())),
                           preferred_element_type=F32)


def _split(x, parts):
    out = []
    for _ in range(parts):
        p = x.astype(BF16)
        out.append(p)
        x = x - p.astype(F32)
    return out


def _dot_lhs_exact(m_bf16, x, parts):
    acc = None
    for p in _split(x, parts):
        t = jnp.dot(m_bf16, p, preferred_element_type=F32)
        acc = t if acc is None else acc + t
    return acc


def _sigmoid(x):
    return 1.0 / (1.0 + jnp.exp(-x))


def _silu(x):
    return x * _sigmoid(x)


def _softplus(x):
    return jnp.maximum(x, 0.0) + jnp.log(1.0 + jnp.exp(-jnp.abs(x)))


def _rms_rows(x, gain):
    return x * lax.rsqrt(jnp.mean(x * x, axis=-1, keepdims=True) + EPS) * gain


def _tri_incl(n):
    r = lax.broadcasted_iota(jnp.int32, (n, n), 0)
    c = lax.broadcasted_iota(jnp.int32, (n, n), 1)
    return r >= c


def _in_proj_kernel(x_ref, g_ref, w_ref, ws_ref, o_ref, os_ref, h_ref):
    @pl.when(pl.program_id(1) == 0)
    def _():
        h = _rms_rows(x_ref[...], g_ref[...]).astype(BF16)
        h_ref[...] = h
        os_ref[...] = jnp.dot(h, ws_ref[...], preferred_element_type=F32)

    o_ref[...] = jnp.dot(h_ref[...], w_ref[...], preferred_element_type=F32).astype(o_ref.dtype)


def _in_proj(x, gain, w_main, w_small):
    t, d = x.shape
    return pl.pallas_call(
        _in_proj_kernel,
        grid=(t // TM_IN, N_MAIN // TN_IN),
        in_specs=[pl.BlockSpec((TM_IN, d), lambda i, j: (i, 0)),
                  pl.BlockSpec((1, d), lambda i, j: (0, 0)),
                  pl.BlockSpec((d, TN_IN), lambda i, j: (0, j)),
                  pl.BlockSpec((d, N_SMALL), lambda i, j: (0, 0))],
        out_specs=[pl.BlockSpec((TM_IN, TN_IN), lambda i, j: (i, j)),
                   pl.BlockSpec((TM_IN, N_SMALL), lambda i, j: (i, 0))],
        out_shape=[jax.ShapeDtypeStruct((t, N_MAIN), BF16), jax.ShapeDtypeStruct((t, N_SMALL), F32)],
        scratch_shapes=[pltpu.VMEM((TM_IN, d), BF16)],
        compiler_params=_cparams(2),
        name="in_proj",
    )(x, gain.reshape(1, d), w_main, w_small)


def _rope_kernel(pos_ref, f_ref, s_ref, cos_ref, sin_ref):
    ang = pos_ref[...] * f_ref[...]
    cos_ref[...] = jnp.cos(ang)
    sin_ref[...] = jnp.sin(ang) * s_ref[...]


def _rope_tables(pos):
    t = pos.shape[0]
    half = RET_DK // 2
    inv = (ROPE_BASE ** (-np.arange(0, RET_DK, 2, dtype=np.float32) / np.float32(RET_DK))).astype(np.float32)
    inv2 = jnp.asarray(np.concatenate([inv, inv]).reshape(1, RET_DK))
    sign = jnp.asarray(np.concatenate([-np.ones(half, np.float32), np.ones(half, np.float32)]).reshape(1, RET_DK))
    rows = TM
    return pl.pallas_call(
        _rope_kernel,
        grid=(t // rows,),
        in_specs=[pl.BlockSpec((rows, 1), lambda i: (i, 0)),
                  pl.BlockSpec((1, RET_DK), lambda i: (0, 0)),
                  pl.BlockSpec((1, RET_DK), lambda i: (0, 0))],
        out_specs=[pl.BlockSpec((rows, RET_DK), lambda i: (i, 0))] * 2,
        out_shape=[jax.ShapeDtypeStruct((t, RET_DK), F32)] * 2,
        compiler_params=_cparams(1),
        name="rope_tables",
    )(pos, inv2, sign)


def _gla_kernel(u_ref, lr_ref, wdh_ref, wdl_ref, bd_ref, ng_ref, o_ref, st_ref, la_ref, *, steps_per_seq):
    @pl.when(pl.program_id(0) % steps_per_seq == 0)
    def _():
        st_ref[...] = jnp.zeros_like(st_ref)

    lr_hi, lr_lo = _split(lr_ref[...], 2)
    z = (jnp.dot(lr_hi, wdh_ref[...], preferred_element_type=F32)
         + (jnp.dot(lr_lo, wdh_ref[...], preferred_element_type=F32)
            + jnp.dot(lr_hi, wdl_ref[...], preferred_element_type=F32))) + bd_ref[...]
    la_ref[...] = -_softplus(-z) * (1.0 / GLA_GATE_NORMALIZER)

    causal = _tri_incl(CHUNK)
    tri = causal.astype(BF16)
    scale = GLA_DK ** -0.5
    ng = ng_ref[...]

    heads = range(GLA_HEADS)
    kcol = [slice(h * GLA_DK, (h + 1) * GLA_DK) for h in heads]
    vcol = [slice(h * GLA_DV, (h + 1) * GLA_DV) for h in heads]

    chunks = range(ROWS // CHUNK)
    units = [(c, h) for c in chunks for h in heads]
    rows = [slice(c * CHUNK, (c + 1) * CHUNK) for c in chunks]
    seg = lambda off, col: [u_ref[rows[c], off + col[h].start:off + col[h].stop] for c, h in units]
    q, k = seg(0, kcol), seg(GLA_QK, kcol)
    vb, g = seg(2 * GLA_QK, vcol), seg(2 * GLA_QK + MIX, vcol)
    cum_c = [_dot_lhs_exact(tri, la_ref[rows[c], :], 2) for c in chunks]
    cum = [cum_c[c][:, kcol[h]] for c, h in units]
    n = range(len(units))
    cl = [cum[i][CHUNK - 1:CHUNK, :] for i in n]
    q_in = [(q[i] * jnp.exp(cum[i]) * scale).astype(BF16) for i in n]
    k_in = [k[i] * jnp.exp(-cum[i]) for i in n]
    k_st = [k[i] * jnp.exp(cl[i] - cum[i]) for i in n]
    sc = [jnp.where(causal, _dot_nt(q_in[i], k_in[i]), 0.0) for i in n]
    kv = [_dot_tn(vb[i], k_st[i]) for i in n]
    o = [_dot(sc[i], vb[i]) for i in n]
    st = [None] * len(units)
    for h in heads:
        cur = st_ref[h]
        for c in chunks:
            i = c * GLA_HEADS + h
            st[i] = cur
            cur = cur * jnp.exp(cl[i]) + kv[i]
        st_ref[h] = cur
    o = [o[i] + _dot_nt(q_in[i], st[i]) for i in n]
    for i, (c, h) in enumerate(units):
        y = o[i] * lax.rsqrt(jnp.mean(o[i] * o[i], axis=-1, keepdims=True) + EPS) * ng
        o_ref[rows[c], vcol[h]] = (y * _silu(g[i].astype(F32))).astype(o_ref.dtype)


def _gla(u, small, wd_pad, b_decay, norm_g, seq):
    t = u.shape[0]
    wd_hi = wd_pad.astype(BF16)
    wd_hi_residual = (wd_pad - wd_hi.astype(F32)).astype(BF16)
    return pl.pallas_call(
        functools.partial(_gla_kernel, steps_per_seq=seq // ROWS),
        grid=(t // ROWS,),
        in_specs=[pl.BlockSpec((ROWS, SEG), lambda i: (i, GATE_COLS // SEG)),
                  pl.BlockSpec((ROWS, LANES), lambda i: (i, SM_LR // LANES)),
                  pl.BlockSpec((LANES, GLA_QK), lambda i: (0, 0)),
                  pl.BlockSpec((LANES, GLA_QK), lambda i: (0, 0)),
                  pl.BlockSpec((1, GLA_QK), lambda i: (0, 0)),
                  pl.BlockSpec((1, GLA_DV), lambda i: (0, 0))],
        out_specs=pl.BlockSpec((ROWS, MIX), lambda i: (i, 0)),
        out_shape=jax.ShapeDtypeStruct((t, MIX), BF16),
        scratch_shapes=[pltpu.VMEM((GLA_HEADS, GLA_DV, GLA_DK), F32),
                        pltpu.VMEM((ROWS, GLA_QK), F32)],
        compiler_params=_cparams(1),
        name="gla_mixer",
    )(u, small, wd_hi, wd_hi_residual, b_decay.reshape(1, GLA_QK), norm_g.reshape(1, GLA_DV))


def _ret_consts():
    lg = np.log(1.0 - np.exp2(-5.0 - np.arange(RET_HEADS, dtype=np.float32))).astype(np.float32)
    pos = np.arange(CHUNK, dtype=np.float32)
    diff = pos[:, None] - pos[None, :]
    causal = diff >= 0
    dmask = np.where(causal, np.exp(lg[:, None, None] * np.where(causal, diff, 0.0)), 0.0).astype(np.float32)
    qs = np.exp(lg[:, None] * (pos[None, :] + 1.0)).astype(np.float32)
    vs = np.exp(lg[:, None] * (CHUNK - 1.0 - pos[None, :])).astype(np.float32)
    cd = [float(np.exp(np.float32(l) * np.float32(CHUNK))) for l in lg]
    qs_b = np.broadcast_to(qs[:, :, None], (RET_HEADS, CHUNK, RET_DK)).copy()
    vs_b = np.broadcast_to(vs[:, :, None], (RET_HEADS, CHUNK, RET_DK)).copy()
    return dmask, qs_b, vs_b, cd


def _ret_kernel(u_ref, cos_ref, sin_ref, dm_ref, qs_ref, vs_ref, o_ref, st_ref, *, steps_per_seq, chunk_decay):
    @pl.when(pl.program_id(0) % steps_per_seq == 0)
    def _():
        st_ref[...] = jnp.zeros_like(st_ref)

    scale = RET_DK ** -0.5
    half = RET_DK // 2

    heads = range(RET_HEADS)
    kcol = [slice(h * RET_DK, (h + 1) * RET_DK) for h in heads]
    vcol = [slice(h * RET_DV, (h + 1) * RET_DV) for h in heads]

    chunks = range(ROWS // CHUNK)
    units = [(c, h) for c in chunks for h in heads]
    rows = [slice(c * CHUNK, (c + 1) * CHUNK) for c in chunks]
    cos = [cos_ref[rows[c], :] for c in chunks]
    sin = [sin_ref[rows[c], :] for c in chunks]
    seg = lambda off, col: [u_ref[rows[c], off + col[h].start:off + col[h].stop] for c, h in units]
    q, k = seg(0, kcol), seg(RET_QK, kcol)
    vb, g = seg(2 * RET_QK, vcol), seg(2 * RET_QK + MIX, vcol)
    n = range(len(units))
    rot = lambda x, c: x * cos[c] + pltpu.roll(x, half, 1) * sin[c]
    q = [rot(q[i].astype(F32), c) * scale for i, (c, h) in enumerate(units)]
    k = [rot(k[i].astype(F32), c) for i, (c, h) in enumerate(units)]
    sc = [_dot_nt(q[i], k[i]) * dm_ref[h] for i, (c, h) in enumerate(units)]
    kv = [_dot_tn(vb[i], k[i] * vs_ref[h]) for i, (c, h) in enumerate(units)]
    o = [_dot(sc[i], vb[i]) for i in n]
    st = [None] * len(units)
    for h in heads:
        cur = st_ref[h]
        for c in chunks:
            i = c * RET_HEADS + h
            st[i] = cur
            cur = cur * chunk_decay[h] + kv[i]
        st_ref[h] = cur
    o = [o[i] + _dot_nt(q[i] * qs_ref[h], st[i]) for i, (c, h) in enumerate(units)]
    for i, (c, h) in enumerate(units):
        y = o[i] * lax.rsqrt(jnp.mean(o[i] * o[i], axis=-1, keepdims=True) + EPS)
        o_ref[rows[c], vcol[h]] = (_silu(g[i].astype(F32)) * y).astype(o_ref.dtype)


def _retention(u, cos2, sin2, seq):
    t = u.shape[0]
    dmask, qs_b, vs_b, cd = _ret_consts()
    return pl.pallas_call(
        functools.partial(_ret_kernel, steps_per_seq=seq // ROWS, chunk_decay=tuple(cd)),
        grid=(t // ROWS,),
        in_specs=[pl.BlockSpec((ROWS, SEG), lambda i: (i, GATE_COLS // SEG + 2)),
                  pl.BlockSpec((ROWS, RET_DK), lambda i: (i, 0)),
                  pl.BlockSpec((ROWS, RET_DK), lambda i: (i, 0)),
                  pl.BlockSpec((RET_HEADS, CHUNK, CHUNK), lambda i: (0, 0, 0)),
                  pl.BlockSpec((RET_HEADS, CHUNK, RET_DK), lambda i: (0, 0, 0)),
                  pl.BlockSpec((RET_HEADS, CHUNK, RET_DK), lambda i: (0, 0, 0))],
        out_specs=pl.BlockSpec((ROWS, MIX), lambda i: (i, 0)),
        out_shape=jax.ShapeDtypeStruct((t, MIX), BF16),
        scratch_shapes=[pltpu.VMEM((RET_HEADS, RET_DV, RET_DK), F32)],
        compiler_params=_cparams(1),
        name="retention_mixer",
    )(u, cos2, sin2, jnp.asarray(dmask), jnp.asarray(qs_b), jnp.asarray(vs_b))


def _rwkv_kernel(*refs, steps_per_seq, first_layer):
    if first_layer:
        (u_ref, sm_ref, mu_ref, mus_ref, w0_ref, w2_ref, a0_ref, a2_ref, g2_ref, kk_ref, ka_ref,
         rk_ref, lnw_ref, lnb_ref, o_ref, vf_out_ref,
         st_ref, cu_ref, cs_ref, r_s, lw_s, k_s, v_s, kk_s, a_s, g_s) = refs
    else:
        (u_ref, sm_ref, mu_ref, mus_ref, w0_ref, w2_ref, a0_ref, a2_ref, g2_ref, kk_ref, ka_ref,
         rk_ref, lnw_ref, lnb_ref, vf_ref, v0_ref, v1_ref, v2_ref, o_ref,
         st_ref, cu_ref, cs_ref, r_s, lw_s, k_s, v_s, kk_s, a_s, g_s) = refs

    @pl.when(pl.program_id(0) % steps_per_seq == 0)
    def _():
        st_ref[...] = jnp.zeros_like(st_ref)
        cu_ref[...] = jnp.zeros_like(cu_ref)
        cs_ref[...] = jnp.zeros_like(cs_ref)

    row = lax.broadcasted_iota(jnp.int32, (ROWS, 1), 0)

    def shifted(x, carry_ref, mu):
        prev = jnp.where(row == 0, carry_ref[0:1, :], pltpu.roll(x, 1, 0))
        carry_ref[0:1, :] = x[ROWS - 1:ROWS, :]
        return x + (prev - x) * mu

    us = shifted(u_ref[...].astype(F32), cu_ref, mu_ref[...])
    ss = shifted(sm_ref[...], cs_ref, mus_ref[...])
    r = us[:, 0:MIX]
    k = us[:, MIX:2 * MIX]
    v = us[:, 2 * MIX:3 * MIX]
    wl = ss[:, SM_WL:SM_WL + LANES]
    al = ss[:, SM_AL:SM_AL + LANES]
    gl = ss[:, SM_GL:SM_GL + 2 * LANES]
    w = w0_ref[...] + _dot(jnp.tanh(wl), w2_ref[...])
    lw_s[...] = -math.exp(-0.5) * _sigmoid(w)
    a = _sigmoid(a0_ref[...] + _dot(al, a2_ref[...]))
    g_s[...] = _dot(_sigmoid(gl), g2_ref[...])
    if first_layer:
        vf_out_ref[...] = v
    else:
        mix = _sigmoid(v0_ref[...] + _dot(_dot(v, v1_ref[...]), v2_ref[...]))
        v = v + (vf_ref[...] - v) * mix
    r_s[...] = r
    v_s[...] = v
    a_s[...] = a
    kk_s[...] = k * kk_ref[...]
    k_s[...] = k * (1.0 + (a - 1.0) * ka_ref[...])

    lane = lax.broadcasted_iota(jnp.int32, (1, LANES), 1)
    mlo = (lane < RWKV_HEAD).astype(F32)
    mhi = 1.0 - mlo
    rr = lax.broadcasted_iota(jnp.int32, (LANES, LANES), 0)
    cc = lax.broadcasted_iota(jnp.int32, (LANES, LANES), 1)
    bd_mask = (rr // RWKV_HEAD) == (cc // RWKV_HEAD)
    bd = bd_mask.astype(BF16)
    eye = (rr == cc).astype(F32)
    tri = _tri_incl(CHUNK).astype(BF16)
    trow = lax.broadcasted_iota(jnp.int32, (CHUNK, LANES), 0)
    scol = lax.broadcasted_iota(jnp.int32, (CHUNK, LANES), 1) % RWKV_HEAD
    strict = scol < trow
    incl = scol <= trow
    inv_n = 1.0 / RWKV_HEAD

    def stack(x):
        return jnp.concatenate([x * mlo, x * mhi], axis=0)

    pairs = range(RWKV_PAIRS)
    cols = [slice(p * LANES, (p + 1) * LANES) for p in pairs]
    n_double = int(math.log2(CHUNK)) - 1

    def head_sums(tiles):
        s_all = jnp.dot(jnp.concatenate(tiles, axis=0).astype(BF16), bd, preferred_element_type=F32)
        return [s_all[p * CHUNK:(p + 1) * CHUNK] for p in range(len(tiles))]

    def prepare(c, ctx):
        rows = slice(c * CHUNK, (c + 1) * CHUNK)
        tile = lambda ref: [ref[rows, cols[p]] for p in pairs]
        r_t, lw_t, k_t, v_t, kkr, a_t = map(tile, (r_s, lw_s, k_s, v_s, kk_s, a_s))
        nrm = head_sums([kkr[p] * kkr[p] for p in pairs])
        cum_all = _dot_lhs_exact(tri, lw_s[rows, :], 2)
        yield
        cum = [cum_all[:, cols[p]] for p in pairs]
        kk = [kkr[p] / jnp.maximum(jnp.sqrt(nrm[p]), 1e-12) for p in pairs]
        bv = [kk[p] * a_t[p] for p in pairs]
        cl = [cum[p][CHUNK - 1:CHUNK, :] for p in pairs]
        e_out = [jnp.exp(-cum[p]) for p in pairs]
        e_end = [jnp.exp(cl[p] - cum[p]) for p in pairs]
        at = [-kk[p] * jnp.exp(cum[p] - lw_t[p]) for p in pairs]
        rt = [r_t[p] * jnp.exp(cum[p]) for p in pairs]
        gm = [_dot_nt(jnp.concatenate([at[p], rt[p]], axis=0),
                      jnp.concatenate([stack(bv[p] * e_out[p]), stack(k_t[p] * e_out[p])], axis=0))
              for p in pairs]
        yield
        ab = [jnp.where(strict, gm[p][0:CHUNK, 0:LANES], 0.0) for p in pairs]
        ak = [jnp.where(strict, gm[p][0:CHUNK, LANES:2 * LANES], 0.0) for p in pairs]
        rb = [jnp.where(incl, gm[p][CHUNK:2 * CHUNK, 0:LANES], 0.0) for p in pairs]
        rk = [jnp.where(incl, gm[p][CHUNK:2 * CHUNK, LANES:2 * LANES], 0.0) for p in pairs]
        vs2 = [stack(v_t[p]) for p in pairs]
        npow = [stack(ab[p]) for p in pairs]
        tinv = [eye + npow[p] for p in pairs]
        npow = [_dot(npow[p], npow[p]) for p in pairs]
        akv = [_dot(ak[p], vs2[p]) for p in pairs]
        rkv = [_dot(rk[p], vs2[p]) for p in pairs]
        yield
        for _ in range(n_double - 1):
            prod = [_dot(tinv[p], npow[p]) for p in pairs]
            npow = [_dot(npow[p], npow[p]) for p in pairs]
            yield
            tinv = [tinv[p] + prod[p] for p in pairs]
        prod = [_dot(tinv[p], npow[p]) for p in pairs]
        rkk = head_sums([r_t[p] * k_t[p] * rk_ref[:, cols[p]] for p in pairs])
        yield
        tinv = [tinv[p] + prod[p] for p in pairs]
        ctx.update(
            tcat=[tinv[p][0:CHUNK] + tinv[p][CHUNK:2 * CHUNK] for p in pairs],
            kb_end=[jnp.concatenate([bv[p] * e_end[p], k_t[p] * e_end[p]], axis=0) for p in pairs],
            bonus=[rkk[p] * v_t[p] for p in pairs],
            at=at, rt=rt, akv=akv, rkv=rkv, rb=rb, cl=cl, v_t=v_t)

    def apply(c, ctx):
        rows = slice(c * CHUNK, (c + 1) * CHUNK)
        st = [st_ref[p] for p in pairs]
        u_rhs = [_dot_nt(ctx["at"][p], st[p]) + ctx["akv"][p] for p in pairs]
        y_st = [_dot_nt(ctx["rt"][p], st[p]) for p in pairs]
        yield
        u_t = [_dot(ctx["tcat"][p], stack(u_rhs[p])) for p in pairs]
        yield
        y = [y_st[p] + _dot(ctx["rb"][p], stack(u_t[p])) + ctx["rkv"][p] for p in pairs]
        upd = [_dot_tn(jnp.concatenate([u_t[p], ctx["v_t"][p]], axis=0), ctx["kb_end"][p]) for p in pairs]
        yield
        for p in pairs:
            st_ref[p] = st[p] * jnp.exp(ctx["cl"][p]) + jnp.where(bd_mask, upd[p], 0.0)
        mean = head_sums(y)
        yield
        dev = [y[p] - mean[p] * inv_n for p in pairs]
        var = head_sums([dev[p] * dev[p] for p in pairs])
        yield
        for p in pairs:
            yn = dev[p] * lax.rsqrt(var[p] * inv_n + RWKV_GN_EPS) * lnw_ref[:, cols[p]] + lnb_ref[:, cols[p]]
            o_ref[rows, cols[p]] = ((yn + ctx["bonus"][p]) * g_s[rows, cols[p]]).astype(o_ref.dtype)

    def run_interleaved(*gens):
        live = list(gens)
        while live:
            for g in list(live):
                if next(g, StopIteration) is StopIteration:
                    live.remove(g)

    n_chunks = ROWS // CHUNK
    ctx = {}
    run_interleaved(prepare(0, ctx))
    for c in range(n_chunks):
        nxt = {}
        if c + 1 < n_chunks:
            run_interleaved(apply(c, ctx), prepare(c + 1, nxt))
        else:
            run_interleaved(apply(c, ctx))
        ctx = nxt


def _rwkv(u, small, prm, v_first, seq):
    t = u.shape[0]
    first = v_first is None
    row = lambda a: a.reshape(1, -1)
    vec = lambda n: pl.BlockSpec((1, n), lambda i: (0, 0))
    mat = lambda r, c: pl.BlockSpec((r, c), lambda i: (0, 0))
    args = [u, small, row(prm["mu"]), row(prm["mu_s"]), row(prm["w0"]), prm["w2"], row(prm["a0"]), prm["a2"],
            prm["g2"], row(prm["k_k"]), row(prm["k_a"]), row(prm["r_k"]), row(prm["ln_w"]), row(prm["ln_b"])]
    in_specs = [pl.BlockSpec((ROWS, SEG), lambda i: (i, GATE_COLS // SEG + 1)),
                pl.BlockSpec((ROWS, 4 * LANES), lambda i: (i, 0)),
                vec(SEG), vec(4 * LANES), vec(MIX), mat(LANES, MIX), vec(MIX), mat(LANES, MIX),
                mat(2 * LANES, MIX), vec(MIX), vec(MIX), vec(MIX), vec(MIX), vec(MIX)]
    out_block = pl.BlockSpec((ROWS, MIX), lambda i: (i, 0))
    if first:
        out_specs = [out_block, out_block]
        out_shape = [jax.ShapeDtypeStruct((t, MIX), BF16), jax.ShapeDtypeStruct((t, MIX), F32)]
    else:
        args += [v_first, row(prm["v0"]), prm["v1"], prm["v2"]]
        in_specs += [out_block, vec(MIX), mat(MIX, LANES), mat(LANES, MIX)]
        out_specs = out_block
        out_shape = jax.ShapeDtypeStruct((t, MIX), BF16)
    tok = pltpu.VMEM((ROWS, MIX), F32)
    res = pl.pallas_call(
        functools.partial(_rwkv_kernel, steps_per_seq=seq // ROWS, first_layer=first),
        grid=(t // ROWS,),
        in_specs=in_specs,
        out_specs=out_specs,
        out_shape=out_shape,
        scratch_shapes=[pltpu.VMEM((RWKV_PAIRS, LANES, LANES), F32),
                        pltpu.VMEM((8, SEG), F32), pltpu.VMEM((8, 4 * LANES), F32),
                        tok, tok, tok, tok, tok, tok, tok],
        compiler_params=_cparams(1),
        name="rwkv7_mixer",
    )(*args)
    return (res[0], res[1]) if first else (res, v_first)


def _merge_kernel(x_ref, oa_ref, ob_ref, oc_ref, g_ref, wa_ref, wb_ref, wc_ref, wo_ref, o_ref):
    d = D_MODEL
    gate = lambda k: _sigmoid(g_ref[:, k * d:(k + 1) * d].astype(F32))
    m = (gate(0) * jnp.dot(oa_ref[...], wa_ref[...], preferred_element_type=F32)
         + gate(1) * jnp.dot(ob_ref[...], wb_ref[...], preferred_element_type=F32)
         + gate(2) * jnp.dot(oc_ref[...], wc_ref[...], preferred_element_type=F32))
    o_ref[...] = x_ref[...] + jnp.dot(m.astype(BF16), wo_ref[...], preferred_element_type=F32)


def _merge_out(x, u, o_a, o_b, o_c, wa, wb, wc, wo):
    t, d = x.shape
    full = pl.BlockSpec((TM_MERGE, d), lambda i: (i, 0))
    mixo = pl.BlockSpec((TM_MERGE, MIX), lambda i: (i, 0))
    resident = lambda r, c: pl.BlockSpec((r, c), lambda i: (0, 0), pipeline_mode=pl.Buffered(1))
    return pl.pallas_call(
        _merge_kernel,
        grid=(t // TM_MERGE,),
        in_specs=[full, mixo, mixo, mixo, pl.BlockSpec((TM_MERGE, GATE_COLS), lambda i: (i, 0)),
                  resident(MIX, d), resident(MIX, d), resident(MIX, d), resident(d, d)],
        out_specs=full,
        out_shape=jax.ShapeDtypeStruct((t, d), F32),
        compiler_params=_cparams(1),
        name="merge_out_proj",
    )(x, o_a, o_b, o_c, u, wa, wb, wc, wo)


def _ffn_kernel(x_ref, g_ref, wz_ref, wv_ref, cw_ref, cb_ref, wd_ref, o_ref, h_ref, carry_ref, *,
                tiles_per_seq):
    i, j = pl.program_id(0), pl.program_id(1)

    @pl.when(j == 0)
    def _():
        x = x_ref[...]
        h_ref[...] = _rms_rows(x, g_ref[...]).astype(BF16)
        o_ref[...] = x

    @pl.when(jnp.logical_and(j == 0, i % tiles_per_seq == 0))
    def _():
        carry_ref[...] = jnp.zeros_like(carry_ref)

    h = h_ref[...]
    z = jnp.dot(h, wz_ref[...], preferred_element_type=F32)
    val = jnp.dot(h, wv_ref[...], preferred_element_type=F32)
    slot = pl.ds(pl.multiple_of(j * 8, 8), 8)
    tail = carry_ref[slot, :]
    carry_ref[slot, :] = z[TM - 8:TM, :]
    row = lax.broadcasted_iota(jnp.int32, (TM, 1), 0)
    z1 = jnp.where(row == 0, tail[7:8, :], pltpu.roll(z, 1, 0))
    z2 = jnp.where(row == 0, tail[6:7, :], jnp.where(row == 1, tail[7:8, :], pltpu.roll(z, 2, 0)))
    cw = cw_ref[...]
    zc = z * cw[2:3, :] + z1 * cw[1:2, :] + z2 * cw[0:1, :] + cb_ref[...]
    gelu = 0.5 * zc * (1.0 + jnp.tanh(math.sqrt(2.0 / math.pi) * (zc + GELU_CUBIC * (zc * zc * zc))))
    o_ref[...] += jnp.dot((gelu * val).astype(BF16), wd_ref[...], preferred_element_type=F32)


def _ffn(x, gain, wz, wv, cw, cb, wd, seq):
    t, d = x.shape
    nj = FF_PAD // TN_FF
    full = pl.BlockSpec((TM, d), lambda i, j: (i, 0))
    return pl.pallas_call(
        functools.partial(_ffn_kernel, tiles_per_seq=seq // TM),
        grid=(t // TM, nj),
        in_specs=[full, pl.BlockSpec((1, d), lambda i, j: (0, 0)),
                  pl.BlockSpec((d, TN_FF), lambda i, j: (0, j)),
                  pl.BlockSpec((d, TN_FF), lambda i, j: (0, j)),
                  pl.BlockSpec((CONV_W, TN_FF), lambda i, j: (0, j)),
                  pl.BlockSpec((1, TN_FF), lambda i, j: (0, j)),
                  pl.BlockSpec((TN_FF, d), lambda i, j: (j, 0))],
        out_specs=full,
        out_shape=jax.ShapeDtypeStruct((t, d), F32),
        scratch_shapes=[pltpu.VMEM((TM, d), BF16), pltpu.VMEM((nj * 8, TN_FF), F32)],
        compiler_params=_cparams(2),
        name="ffn_geglu_conv",
    )(x, gain.reshape(1, d), wz, wv, cw, cb.reshape(1, FF_PAD), wd)


def _ple_kernel(x_ref, g_ref, wg_ref, p_ref, wp_ref, fg_ref, o_ref, *, final):
    x = x_ref[...]
    h = _rms_rows(x, g_ref[...])
    gate = _sigmoid(_dot(h, wg_ref[...]))
    y = x + gate * _dot(p_ref[...], wp_ref[...])
    if final:
        y = _rms_rows(y, fg_ref[...])
    o_ref[...] = y


def _ple(x, gain, wg, p, wp, final_gain, final):
    t, d = x.shape
    full = pl.BlockSpec((TM, d), lambda i: (i, 0))
    vec = pl.BlockSpec((1, d), lambda i: (0, 0))
    return pl.pallas_call(
        functools.partial(_ple_kernel, final=final),
        grid=(t // TM,),
        in_specs=[full, vec, pl.BlockSpec((d, d), lambda i: (0, 0)),
                  pl.BlockSpec((TM, PLE_DIM), lambda i: (i, 0)),
                  pl.BlockSpec((PLE_DIM, d), lambda i: (0, 0)), vec],
        out_specs=full,
        out_shape=jax.ShapeDtypeStruct((t, d), F32),
        compiler_params=_cparams(1),
        name="ple_embed",
    )(x, gain.reshape(1, d), wg, p, wp, final_gain.reshape(1, d))


def _pad_cols(a, n):
    return jnp.pad(a, ((0, 0), (0, n - a.shape[1])))


def _pad_rows(a, n):
    return jnp.pad(a, ((0, n - a.shape[0]), (0, 0)))


def _pad_vec(a, n):
    return jnp.pad(a, (0, n - a.shape[0]))


def _layout_w_in(w):
    o_rwkv = GLA_COLS
    o_ret = GLA_COLS + RWKV_COLS
    o_gate = o_ret + RET_COLS
    main = jnp.concatenate([w[:, o_gate:o_gate + GATE_COLS], w[:, 0:SEG],
                            w[:, o_rwkv:o_rwkv + SEG], w[:, o_ret:o_ret + SEG]], axis=1)
    o = o_rwkv + SEG
    small = jnp.concatenate([
        _pad_cols(w[:, o:o + RWKV_W_LORA], LANES),
        _pad_cols(w[:, o + RWKV_W_LORA:o + RWKV_W_LORA + RWKV_A_LORA], LANES),
        _pad_cols(w[:, o + RWKV_W_LORA + RWKV_A_LORA:o + RWKV_W_LORA + RWKV_A_LORA + RWKV_G_LORA], 2 * LANES),
        _pad_cols(w[:, SEG:SEG + GLA_LORA], LANES)], axis=1)
    return main.astype(BF16), small.astype(BF16)


def _layout_mu(mu):
    o = SEG
    small = jnp.concatenate([
        _pad_vec(mu[o:o + RWKV_W_LORA], LANES),
        _pad_vec(mu[o + RWKV_W_LORA:o + RWKV_W_LORA + RWKV_A_LORA], LANES),
        _pad_vec(mu[o + RWKV_W_LORA + RWKV_A_LORA:], 2 * LANES)])
    return mu[:SEG], small


def kernel(x, p, positions, mix_norm, w_in, gla_w_decay, gla_b_decay, gla_norm, rwkv_mu, rwkv_w0, rwkv_w2, rwkv_a0, rwkv_a2, rwkv_g2, rwkv_k_k, rwkv_k_a, rwkv_r_k, rwkv_ln_w, rwkv_ln_b, rwkv_v0, rwkv_v1, rwkv_v2, w_br_gla, w_br_rwkv, w_br_ret, w_o, ffn_norm, w_up, conv_w, conv_b, w_down, ple_norm, w_ple_gate, w_ple_proj, final_norm):
    b, s, d = x.shape
    depth = w_in.shape[0]
    t = b * s
    assert d == D_MODEL and s % TM == 0 and s % ROWS == 0 and t % TM_IN == 0
    xf = x.reshape(t, d)
    cos2, sin2 = _rope_tables(positions.astype(F32).reshape(t, 1))
    v_first = None
    for i in range(depth):
        w_main, w_small = _layout_w_in(w_in[i])
        u, small = _in_proj(xf, mix_norm[i], w_main, w_small)
        o_a = _gla(u, small, _pad_rows(gla_w_decay[i], LANES), gla_b_decay[i], gla_norm[i], s)
        mu_main, mu_small = _layout_mu(rwkv_mu[i])
        prm = dict(mu=mu_main, mu_s=mu_small, w0=rwkv_w0[i], w2=_pad_rows(rwkv_w2[i], LANES).astype(BF16),
                   a0=rwkv_a0[i], a2=_pad_rows(rwkv_a2[i], LANES).astype(BF16),
                   g2=_pad_rows(rwkv_g2[i], 2 * LANES).astype(BF16), k_k=rwkv_k_k[i], k_a=rwkv_k_a[i],
                   r_k=rwkv_r_k[i].reshape(MIX), ln_w=rwkv_ln_w[i], ln_b=rwkv_ln_b[i])
        if i > 0:
            prm.update(v0=rwkv_v0[i - 1], v1=_pad_cols(rwkv_v1[i - 1], LANES).astype(BF16),
                       v2=_pad_rows(rwkv_v2[i - 1], LANES).astype(BF16))
        o_b, v_first = _rwkv(u, small, prm, v_first, s)
        o_c = _retention(u, cos2, sin2, s)
        xf = _merge_out(xf, u, o_a, o_b, o_c, w_br_gla[i].astype(BF16), w_br_rwkv[i].astype(BF16),
                        w_br_ret[i].astype(BF16), w_o[i].astype(BF16))
        wz = _pad_cols(w_up[i][:, :D_FF], FF_PAD).astype(BF16)
        wv = _pad_cols(w_up[i][:, D_FF:], FF_PAD).astype(BF16)
        xf = _ffn(xf, ffn_norm[i], wz, wv, _pad_cols(conv_w[i], FF_PAD), _pad_vec(conv_b[i], FF_PAD),
                  _pad_rows(w_down[i], FF_PAD).astype(BF16), s)
        xf = _ple(xf, ple_norm[i], w_ple_gate[i].astype(BF16), p[i].reshape(t, PLE_DIM),
                  w_ple_proj[i].astype(BF16), final_norm, final=(i == depth - 1))
    return xf.reshape(b, s, d)
```

```python
import functools
import math

import numpy as np
import jax
import jax.numpy as jnp
from jax import lax
from jax.experimental import pallas as pl
from jax.experimental.pallas import tpu as pltpu

F32 = jnp.float32
BF16 = jnp.bfloat16

D_MODEL = 2048
PLE_DIM = 256
MIX = D_MODEL // 2
CHUNK = 64
EPS = 1e-6
GLA_HEADS, GLA_DK, GLA_DV, GLA_LORA = 4, 128, 256, 16
GLA_GATE_NORMALIZER = 16.0
RWKV_HEAD = 64
RWKV_PAIRS = MIX // (2 * RWKV_HEAD)
RWKV_W_LORA, RWKV_A_LORA, RWKV_V_LORA, RWKV_G_LORA = 64, 64, 32, 160
RWKV_GN_EPS = 64e-5
RET_HEADS, RET_DK, RET_DV = 4, 128, 256
ROPE_BASE = 10000.0
D_FF = 5504
CONV_W = 3
GELU_CUBIC = 0.044715
GLA_QK = GLA_HEADS * GLA_DK
RET_QK = RET_HEADS * RET_DK
GLA_COLS = 2 * GLA_QK + 2 * MIX + GLA_LORA
RWKV_COLS = 3 * MIX + RWKV_W_LORA + RWKV_A_LORA + RWKV_G_LORA
RET_COLS = 2 * RET_QK + 2 * MIX
GATE_COLS = 3 * D_MODEL

LANES = 128
VMEM_LIMIT = 56 * 1024 * 1024

SEG = 3 * MIX
N_MAIN = GATE_COLS + 3 * SEG
SM_WL, SM_AL, SM_GL, SM_LR = 0, 128, 256, 512
N_SMALL = 640
FF_PAD = 5632

TM = 512
TM_IN = 512
TN_IN = 3072
TM_MERGE = 256
TM_FF = 1024
TN_FF = 512
ROWS = 256


def _cparams(n_axes):
    return pltpu.CompilerParams(dimension_semantics=("arbitrary",) * n_axes,
                                vmem_limit_bytes=VMEM_LIMIT)


def _dot(a, b):
    return jnp.dot(a.astype(BF16), b.astype(BF16), preferred_element_type=F32)


def _dot_nt(a, b):
    return lax.dot_general(a.astype(BF16), b.astype(BF16), (((1,), (1,)), ((), ())),
                           preferred_element_type=F32)


def _dot_tn(a, b):
    return lax.dot_general(a.astype(BF16), b.astype(BF16), (((0,), (0,)), ((), ())),
                           preferred_element_type=F32)


def _split(x, parts):
    out = []
    for _ in range(parts):
        p = x.astype(BF16)
        out.append(p)
        x = x - p.astype(F32)
    return out


def _dot_lhs_exact(m_bf16, x, parts):
    acc = None
    for p in _split(x, parts):
        t = jnp.dot(m_bf16, p, preferred_element_type=F32)
        acc = t if acc is None else acc + t
    return acc


def _sigmoid(x):
    return 1.0 / (1.0 + jnp.exp(-x))


def _silu(x):
    return x * _sigmoid(x)


def _softplus(x):
    return jnp.maximum(x, 0.0) + jnp.log(1.0 + jnp.exp(-jnp.abs(x)))


def _rms_rows(x, gain):
    return x * lax.rsqrt(jnp.mean(x * x, axis=-1, keepdims=True) + EPS) * gain


def _tri_incl(n):
    r = lax.broadcasted_iota(jnp.int32, (n, n), 0)
    c = lax.broadcasted_iota(jnp.int32, (n, n), 1)
    return r >= c


def _in_proj_kernel(x_ref, g_ref, w_ref, ws_ref, o_ref, os_ref, h_ref):
    @pl.when(pl.program_id(1) == 0)
    def _():
        h = _rms_rows(x_ref[...], g_ref[...]).astype(BF16)
        h_ref[...] = h
        os_ref[...] = jnp.dot(h, ws_ref[...], preferred_element_type=F32)

    o_ref[...] = jnp.dot(h_ref[...], w_ref[...], preferred_element_type=F32).astype(o_ref.dtype)


def _in_proj(x, gain, w_main, w_small):
    t, d = x.shape
    return pl.pallas_call(
        _in_proj_kernel,
        grid=(t // TM_IN, N_MAIN // TN_IN),
        in_specs=[pl.BlockSpec((TM_IN, d), lambda i, j: (i, 0)),
                  pl.BlockSpec((1, d), lambda i, j: (0, 0)),
                  pl.BlockSpec((d, TN_IN), lambda i, j: (0, j)),
                  pl.BlockSpec((d, N_SMALL), lambda i, j: (0, 0))],
        out_specs=[pl.BlockSpec((TM_IN, TN_IN), lambda i, j: (i, j)),
                   pl.BlockSpec((TM_IN, N_SMALL), lambda i, j: (i, 0))],
        out_shape=[jax.ShapeDtypeStruct((t, N_MAIN), BF16), jax.ShapeDtypeStruct((t, N_SMALL), F32)],
        scratch_shapes=[pltpu.VMEM((TM_IN, d), BF16)],
        compiler_params=_cparams(2),
        name="in_proj",
    )(x, gain.reshape(1, d), w_main, w_small)


def _rope_kernel(pos_ref, f_ref, s_ref, cos_ref, sin_ref):
    ang = pos_ref[...] * f_ref[...]
    cos_ref[...] = jnp.cos(ang)
    sin_ref[...] = jnp.sin(ang) * s_ref[...]


def _rope_tables(pos):
    t = pos.shape[0]
    half = RET_DK // 2
    inv = (ROPE_BASE ** (-np.arange(0, RET_DK, 2, dtype=np.float32) / np.float32(RET_DK))).astype(np.float32)
    inv2 = jnp.asarray(np.concatenate([inv, inv]).reshape(1, RET_DK))
    sign = jnp.asarray(np.concatenate([-np.ones(half, np.float32), np.ones(half, np.float32)]).reshape(1, RET_DK))
    rows = TM
    return pl.pallas_call(
        _rope_kernel,
        grid=(t // rows,),
        in_specs=[pl.BlockSpec((rows, 1), lambda i: (i, 0)),
                  pl.BlockSpec((1, RET_DK), lambda i: (0, 0)),
                  pl.BlockSpec((1, RET_DK), lambda i: (0, 0))],
        out_specs=[pl.BlockSpec((rows, RET_DK), lambda i: (i, 0))] * 2,
        out_shape=[jax.ShapeDtypeStruct((t, RET_DK), F32)] * 2,
        compiler_params=_cparams(1),
        name="rope_tables",
    )(pos, inv2, sign)


def _gla_kernel(u_ref, lr_ref, wdh_ref, wdl_ref, bd_ref, ng_ref, o_ref, st_ref, la_ref, *, steps_per_seq):
    @pl.when(pl.program_id(0) % steps_per_seq == 0)
    def _():
        st_ref[...] = jnp.zeros_like(st_ref)

    lr_hi, lr_lo = _split(lr_ref[...], 2)
    z = (jnp.dot(lr_hi, wdh_ref[...], preferred_element_type=F32)
         + (jnp.dot(lr_lo, wdh_ref[...], preferred_element_type=F32)
            + jnp.dot(lr_hi, wdl_ref[...], preferred_element_type=F32))) + bd_ref[...]
    la_ref[...] = -_softplus(-z) * (1.0 / GLA_GATE_NORMALIZER)

    causal = _tri_incl(CHUNK)
    tri = causal.astype(BF16)
    scale = GLA_DK ** -0.5
    ng = ng_ref[...]

    heads = range(GLA_HEADS)
    kcol = [slice(h * GLA_DK, (h + 1) * GLA_DK) for h in heads]
    vcol = [slice(h * GLA_DV, (h + 1) * GLA_DV) for h in heads]

    chunks = range(ROWS // CHUNK)
    units = [(c, h) for c in chunks for h in heads]
    rows = [slice(c * CHUNK, (c + 1) * CHUNK) for c in chunks]
    seg = lambda off, col: [u_ref[rows[c], off + col[h].start:off + col[h].stop] for c, h in units]
    q, k = seg(0, kcol), seg(GLA_QK, kcol)
    vb, g = seg(2 * GLA_QK, vcol), seg(2 * GLA_QK + MIX, vcol)
    cum_c = [_dot_lhs_exact(tri, la_ref[rows[c], :], 2) for c in chunks]
    cum = [cum_c[c][:, kcol[h]] for c, h in units]
    n = range(len(units))
    cl = [cum[i][CHUNK - 1:CHUNK, :] for i in n]
    q_in = [(q[i] * jnp.exp(cum[i]) * scale).astype(BF16) for i in n]
    k_in = [k[i] * jnp.exp(-cum[i]) for i in n]
    k_st = [k[i] * jnp.exp(cl[i] - cum[i]) for i in n]
    sc = [jnp.where(causal, _dot_nt(q_in[i], k_in[i]), 0.0) for i in n]
    kv = [_dot_tn(vb[i], k_st[i]) for i in n]
    o = [_dot(sc[i], vb[i]) for i in n]
    st = [None] * len(units)
    for h in heads:
        cur = st_ref[h]
        for c in chunks:
            i = c * GLA_HEADS + h
            st[i] = cur
            cur = cur * jnp.exp(cl[i]) + kv[i]
        st_ref[h] = cur
    o = [o[i] + _dot_nt(q_in[i], st[i]) for i in n]
    for i, (c, h) in enumerate(units):
        y = o[i] * lax.rsqrt(jnp.mean(o[i] * o[i], axis=-1, keepdims=True) + EPS) * ng
        o_ref[rows[c], vcol[h]] = (y * _silu(g[i].astype(F32))).astype(o_ref.dtype)


def _gla(u, small, wd_pad, b_decay, norm_g, seq):
    t = u.shape[0]
    wd_hi = wd_pad.astype(BF16)
    wd_hi_residual = (wd_pad - wd_hi.astype(F32)).astype(BF16)
    return pl.pallas_call(
        functools.partial(_gla_kernel, steps_per_seq=seq // ROWS),
        grid=(t // ROWS,),
        in_specs=[pl.BlockSpec((ROWS, SEG), lambda i: (i, GATE_COLS // SEG)),
                  pl.BlockSpec((ROWS, LANES), lambda i: (i, SM_LR // LANES)),
                  pl.BlockSpec((LANES, GLA_QK), lambda i: (0, 0)),
                  pl.BlockSpec((LANES, GLA_QK), lambda i: (0, 0)),
                  pl.BlockSpec((1, GLA_QK), lambda i: (0, 0)),
                  pl.BlockSpec((1, GLA_DV), lambda i: (0, 0))],
        out_specs=pl.BlockSpec((ROWS, MIX), lambda i: (i, 0)),
        out_shape=jax.ShapeDtypeStruct((t, MIX), BF16),
        scratch_shapes=[pltpu.VMEM((GLA_HEADS, GLA_DV, GLA_DK), F32),
                        pltpu.VMEM((ROWS, GLA_QK), F32)],
        compiler_params=_cparams(1),
        name="gla_mixer",
    )(u, small, wd_hi, wd_hi_residual, b_decay.reshape(1, GLA_QK), norm_g.reshape(1, GLA_DV))


def _ret_consts():
    lg = np.log(1.0 - np.exp2(-5.0 - np.arange(RET_HEADS, dtype=np.float32))).astype(np.float32)
    pos = np.arange(CHUNK, dtype=np.float32)
    diff = pos[:, None] - pos[None, :]
    causal = diff >= 0
    dmask = np.where(causal, np.exp(lg[:, None, None] * np.where(causal, diff, 0.0)), 0.0).astype(np.float32)
    qs = np.exp(lg[:, None] * (pos[None, :] + 1.0)).astype(np.float32)
    vs = np.exp(lg[:, None] * (CHUNK - 1.0 - pos[None, :])).astype(np.float32)
    cd = [float(np.exp(np.float32(l) * np.float32(CHUNK))) for l in lg]
    qs_b = np.broadcast_to(qs[:, :, None], (RET_HEADS, CHUNK, RET_DK)).copy()
    vs_b = np.broadcast_to(vs[:, :, None], (RET_HEADS, CHUNK, RET_DK)).copy()
    return dmask, qs_b, vs_b, cd


def _ret_kernel(u_ref, cos_ref, sin_ref, dm_ref, qs_ref, vs_ref, o_ref, st_ref, *, steps_per_seq, chunk_decay):
    @pl.when(pl.program_id(0) % steps_per_seq == 0)
    def _():
        st_ref[...] = jnp.zeros_like(st_ref)

    scale = RET_DK ** -0.5
    half = RET_DK // 2

    heads = range(RET_HEADS)
    kcol = [slice(h * RET_DK, (h + 1) * RET_DK) for h in heads]
    vcol = [slice(h * RET_DV, (h + 1) * RET_DV) for h in heads]

    chunks = range(ROWS // CHUNK)
    units = [(c, h) for c in chunks for h in heads]
    rows = [slice(c * CHUNK, (c + 1) * CHUNK) for c in chunks]
    cos = [cos_ref[rows[c], :] for c in chunks]
    sin = [sin_ref[rows[c], :] for c in chunks]
    seg = lambda off, col: [u_ref[rows[c], off + col[h].start:off + col[h].stop] for c, h in units]
    q, k = seg(0, kcol), seg(RET_QK, kcol)
    vb, g = seg(2 * RET_QK, vcol), seg(2 * RET_QK + MIX, vcol)
    n = range(len(units))
    rot = lambda x, c: x * cos[c] + pltpu.roll(x, half, 1) * sin[c]
    q = [rot(q[i].astype(F32), c) * scale for i, (c, h) in enumerate(units)]
    k = [rot(k[i].astype(F32), c) for i, (c, h) in enumerate(units)]
    sc = [_dot_nt(q[i], k[i]) * dm_ref[h] for i, (c, h) in enumerate(units)]
    kv = [_dot_tn(vb[i], k[i] * vs_ref[h]) for i, (c, h) in enumerate(units)]
    o = [_dot(sc[i], vb[i]) for i in n]
    st = [None] * len(units)
    for h in heads:
        cur = st_ref[h]
        for c in chunks:
            i = c * RET_HEADS + h
            st[i] = cur
            cur = cur * chunk_decay[h] + kv[i]
        st_ref[h] = cur
    o = [o[i] + _dot_nt(q[i] * qs_ref[h], st[i]) for i, (c, h) in enumerate(units)]
    for i, (c, h) in enumerate(units):
        y = o[i] * lax.rsqrt(jnp.mean(o[i] * o[i], axis=-1, keepdims=True) + EPS)
        o_ref[rows[c], vcol[h]] = (_silu(g[i].astype(F32)) * y).astype(o_ref.dtype)


def _retention(u, cos2, sin2, seq):
    t = u.shape[0]
    dmask, qs_b, vs_b, cd = _ret_consts()
    return pl.pallas_call(
        functools.partial(_ret_kernel, steps_per_seq=seq // ROWS, chunk_decay=tuple(cd)),
        grid=(t // ROWS,),
        in_specs=[pl.BlockSpec((ROWS, SEG), lambda i: (i, GATE_COLS // SEG + 2)),
                  pl.BlockSpec((ROWS, RET_DK), lambda i: (i, 0)),
                  pl.BlockSpec((ROWS, RET_DK), lambda i: (i, 0)),
                  pl.BlockSpec((RET_HEADS, CHUNK, CHUNK), lambda i: (0, 0, 0)),
                  pl.BlockSpec((RET_HEADS, CHUNK, RET_DK), lambda i: (0, 0, 0)),
                  pl.BlockSpec((RET_HEADS, CHUNK, RET_DK), lambda i: (0, 0, 0))],
        out_specs=pl.BlockSpec((ROWS, MIX), lambda i: (i, 0)),
        out_shape=jax.ShapeDtypeStruct((t, MIX), BF16),
        scratch_shapes=[pltpu.VMEM((RET_HEADS, RET_DV, RET_DK), F32)],
        compiler_params=_cparams(1),
        name="retention_mixer",
    )(u, cos2, sin2, jnp.asarray(dmask), jnp.asarray(qs_b), jnp.asarray(vs_b))


def _rwkv_kernel(*refs, steps_per_seq, first_layer):
    if first_layer:
        (u_ref, sm_ref, mu_ref, mus_ref, w0_ref, w2_ref, a0_ref, a2_ref, g2_ref, kk_ref, ka_ref,
         rk_ref, lnw_ref, lnb_ref, o_ref, vf_out_ref,
         st_ref, cu_ref, cs_ref, r_s, lw_s, k_s, v_s, kk_s, a_s, g_s) = refs
    else:
        (u_ref, sm_ref, mu_ref, mus_ref, w0_ref, w2_ref, a0_ref, a2_ref, g2_ref, kk_ref, ka_ref,
         rk_ref, lnw_ref, lnb_ref, vf_ref, v0_ref, v1_ref, v2_ref, o_ref,
         st_ref, cu_ref, cs_ref, r_s, lw_s, k_s, v_s, kk_s, a_s, g_s) = refs

    @pl.when(pl.program_id(0) % steps_per_seq == 0)
    def _():
        st_ref[...] = jnp.zeros_like(st_ref)
        cu_ref[...] = jnp.zeros_like(cu_ref)
        cs_ref[...] = jnp.zeros_like(cs_ref)

    row = lax.broadcasted_iota(jnp.int32, (ROWS, 1), 0)

    def shifted(x, carry_ref, mu):
        prev = jnp.where(row == 0, carry_ref[0:1, :], pltpu.roll(x, 1, 0))
        carry_ref[0:1, :] = x[ROWS - 1:ROWS, :]
        return x + (prev - x) * mu

    us = shifted(u_ref[...].astype(F32), cu_ref, mu_ref[...])
    ss = shifted(sm_ref[...], cs_ref, mus_ref[...])
    r = us[:, 0:MIX]
    k = us[:, MIX:2 * MIX]
    v = us[:, 2 * MIX:3 * MIX]
    wl = ss[:, SM_WL:SM_WL + LANES]
    al = ss[:, SM_AL:SM_AL + LANES]
    gl = ss[:, SM_GL:SM_GL + 2 * LANES]
    w = w0_ref[...] + _dot(jnp.tanh(wl), w2_ref[...])
    lw_s[...] = -math.exp(-0.5) * _sigmoid(w)
    a = _sigmoid(a0_ref[...] + _dot(al, a2_ref[...]))
    g_s[...] = _dot(_sigmoid(gl), g2_ref[...])
    if first_layer:
        vf_out_ref[...] = v
    else:
        mix = _sigmoid(v0_ref[...] + _dot(_dot(v, v1_ref[...]), v2_ref[...]))
        v = v + (vf_ref[...] - v) * mix
    r_s[...] = r
    v_s[...] = v
    a_s[...] = a
    kk_s[...] = k * kk_ref[...]
    k_s[...] = k * (1.0 + (a - 1.0) * ka_ref[...])

    lane = lax.broadcasted_iota(jnp.int32, (1, LANES), 1)
    mlo = (lane < RWKV_HEAD).astype(F32)
    mhi = 1.0 - mlo
    rr = lax.broadcasted_iota(jnp.int32, (LANES, LANES), 0)
    cc = lax.broadcasted_iota(jnp.int32, (LANES, LANES), 1)
    bd_mask = (rr // RWKV_HEAD) == (cc // RWKV_HEAD)
    bd = bd_mask.astype(BF16)
    eye = (rr == cc).astype(F32)
    tri = _tri_incl(CHUNK).astype(BF16)
    trow = lax.broadcasted_iota(jnp.int32, (CHUNK, LANES), 0)
    scol = lax.broadcasted_iota(jnp.int32, (CHUNK, LANES), 1) % RWKV_HEAD
    strict = scol < trow
    incl = scol <= trow
    inv_n = 1.0 / RWKV_HEAD

    def stack(x):
        return jnp.concatenate([x * mlo, x * mhi], axis=0)

    pairs = range(RWKV_PAIRS)
    cols = [slice(p * LANES, (p + 1) * LANES) for p in pairs]
    n_double = int(math.log2(CHUNK)) - 1

    def head_sums(tiles):
        s_all = jnp.dot(jnp.concatenate(tiles, axis=0).astype(BF16), bd, preferred_element_type=F32)
        return [s_all[p * CHUNK:(p + 1) * CHUNK] for p in range(len(tiles))]

    def prepare(c, ctx):
        rows = slice(c * CHUNK, (c + 1) * CHUNK)
        tile = lambda ref: [ref[rows, cols[p]] for p in pairs]
        r_t, lw_t, k_t, v_t, kkr, a_t = map(tile, (r_s, lw_s, k_s, v_s, kk_s, a_s))
        nrm = head_sums([kkr[p] * kkr[p] for p in pairs])
        cum_all = _dot_lhs_exact(tri, lw_s[rows, :], 2)
        yield
        cum = [cum_all[:, cols[p]] for p in pairs]
        kk = [kkr[p] * lax.rsqrt(jnp.maximum(nrm[p], 1e-24)) for p in pairs]
        bv = [kk[p] * a_t[p] for p in pairs]
        cl = [cum[p][CHUNK - 1:CHUNK, :] for p in pairs]
        e_out = [jnp.exp(-cum[p]) for p in pairs]
        e_end = [jnp.exp(cl[p] - cum[p]) for p in pairs]
        at = [-kk[p] * jnp.exp(cum[p] - lw_t[p]) for p in pairs]
        rt = [r_t[p] * jnp.exp(cum[p]) for p in pairs]
        gm = [_dot_nt(jnp.concatenate([at[p], rt[p]], axis=0),
                      jnp.concatenate([stack(bv[p] * e_out[p]), stack(k_t[p] * e_out[p])], axis=0))
              for p in pairs]
        yield
        ab = [jnp.where(strict, gm[p][0:CHUNK, 0:LANES], 0.0) for p in pairs]
        ak = [jnp.where(strict, gm[p][0:CHUNK, LANES:2 * LANES], 0.0) for p in pairs]
        rb = [jnp.where(incl, gm[p][CHUNK:2 * CHUNK, 0:LANES], 0.0) for p in pairs]
        rk = [jnp.where(incl, gm[p][CHUNK:2 * CHUNK, LANES:2 * LANES], 0.0) for p in pairs]
        vs2 = [stack(v_t[p]) for p in pairs]
        npow = [stack(ab[p]) for p in pairs]
        tinv = [eye + npow[p] for p in pairs]
        npow = [_dot(npow[p], npow[p]) for p in pairs]
        akv = [_dot(ak[p], vs2[p]) for p in pairs]
        rkv = [_dot(rk[p], vs2[p]) for p in pairs]
        yield
        for _ in range(n_double - 1):
            prod = [_dot(tinv[p], npow[p]) for p in pairs]
            npow = [_dot(npow[p], npow[p]) for p in pairs]
            yield
            tinv = [tinv[p] + prod[p] for p in pairs]
        prod = [_dot(tinv[p], npow[p]) for p in pairs]
        rkk = head_sums([r_t[p] * k_t[p] * rk_ref[:, cols[p]] for p in pairs])
        yield
        tinv = [tinv[p] + prod[p] for p in pairs]
        ctx.update(
            tcat=[tinv[p][0:CHUNK] + tinv[p][CHUNK:2 * CHUNK] for p in pairs],
            kb_end=[jnp.concatenate([bv[p] * e_end[p], k_t[p] * e_end[p]], axis=0) for p in pairs],
            bonus=[rkk[p] * v_t[p] for p in pairs],
            at=at, rt=rt, akv=akv, rkv=rkv, rb=rb, cl=cl, v_t=v_t)

    def apply(c, ctx):
        rows = slice(c * CHUNK, (c + 1) * CHUNK)
        st = [st_ref[p] for p in pairs]
        u_rhs = [_dot_nt(ctx["at"][p], st[p]) + ctx["akv"][p] for p in pairs]
        y_st = [_dot_nt(ctx["rt"][p], st[p]) for p in pairs]
        yield
        u_t = [_dot(ctx["tcat"][p], stack(u_rhs[p])) for p in pairs]
        yield
        y = [y_st[p] + _dot(ctx["rb"][p], stack(u_t[p])) + ctx["rkv"][p] for p in pairs]
        upd = [_dot_tn(jnp.concatenate([u_t[p], ctx["v_t"][p]], axis=0), ctx["kb_end"][p]) for p in pairs]
        yield
        for p in pairs:
            st_ref[p] = st[p] * jnp.exp(ctx["cl"][p]) + jnp.where(bd_mask, upd[p], 0.0)
        mean = head_sums(y)
        yield
        dev = [y[p] - mean[p] * inv_n for p in pairs]
        var = head_sums([dev[p] * dev[p] for p in pairs])
        yield
        for p in pairs:
            yn = dev[p] * lax.rsqrt(var[p] * inv_n + RWKV_GN_EPS) * lnw_ref[:, cols[p]] + lnb_ref[:, cols[p]]
            o_ref[rows, cols[p]] = ((yn + ctx["bonus"][p]) * g_s[rows, cols[p]]).astype(o_ref.dtype)

    def run_interleaved(*gens):
        live = list(gens)
        while live:
            for g in list(live):
                if next(g, StopIteration) is StopIteration:
                    live.remove(g)

    n_chunks = ROWS // CHUNK
    ctx = {}
    run_interleaved(prepare(0, ctx))
    for c in range(n_chunks):
        nxt = {}
        if c + 1 < n_chunks:
            run_interleaved(apply(c, ctx), prepare(c + 1, nxt))
        else:
            run_interleaved(apply(c, ctx))
        ctx = nxt


def _rwkv(u, small, prm, v_first, seq):
    t = u.shape[0]
    first = v_first is None
    row = lambda a: a.reshape(1, -1)
    vec = lambda n: pl.BlockSpec((1, n), lambda i: (0, 0))
    mat = lambda r, c: pl.BlockSpec((r, c), lambda i: (0, 0))
    args = [u, small, row(prm["mu"]), row(prm["mu_s"]), row(prm["w0"]), prm["w2"], row(prm["a0"]), prm["a2"],
            prm["g2"], row(prm["k_k"]), row(prm["k_a"]), row(prm["r_k"]), row(prm["ln_w"]), row(prm["ln_b"])]
    in_specs = [pl.BlockSpec((ROWS, SEG), lambda i: (i, GATE_COLS // SEG + 1)),
                pl.BlockSpec((ROWS, 4 * LANES), lambda i: (i, 0)),
                vec(SEG), vec(4 * LANES), vec(MIX), mat(LANES, MIX), vec(MIX), mat(LANES, MIX),
                mat(2 * LANES, MIX), vec(MIX), vec(MIX), vec(MIX), vec(MIX), vec(MIX)]
    out_block = pl.BlockSpec((ROWS, MIX), lambda i: (i, 0))
    if first:
        out_specs = [out_block, out_block]
        out_shape = [jax.ShapeDtypeStruct((t, MIX), BF16), jax.ShapeDtypeStruct((t, MIX), F32)]
    else:
        args += [v_first, row(prm["v0"]), prm["v1"], prm["v2"]]
        in_specs += [out_block, vec(MIX), mat(MIX, LANES), mat(LANES, MIX)]
        out_specs = out_block
        out_shape = jax.ShapeDtypeStruct((t, MIX), BF16)
    tok = pltpu.VMEM((ROWS, MIX), F32)
    res = pl.pallas_call(
        functools.partial(_rwkv_kernel, steps_per_seq=seq // ROWS, first_layer=first),
        grid=(t // ROWS,),
        in_specs=in_specs,
        out_specs=out_specs,
        out_shape=out_shape,
        scratch_shapes=[pltpu.VMEM((RWKV_PAIRS, LANES, LANES), F32),
                        pltpu.VMEM((8, SEG), F32), pltpu.VMEM((8, 4 * LANES), F32),
                        tok, tok, tok, tok, tok, tok, tok],
        compiler_params=_cparams(1),
        name="rwkv7_mixer",
    )(*args)
    return (res[0], res[1]) if first else (res, v_first)


def _merge_kernel(x_ref, oa_ref, ob_ref, oc_ref, g_ref, wa_ref, wb_ref, wc_ref, wo_ref, o_ref):
    d = D_MODEL
    gate = lambda k: _sigmoid(g_ref[:, k * d:(k + 1) * d].astype(F32))
    m = (gate(0) * jnp.dot(oa_ref[...], wa_ref[...], preferred_element_type=F32)
         + gate(1) * jnp.dot(ob_ref[...], wb_ref[...], preferred_element_type=F32)
         + gate(2) * jnp.dot(oc_ref[...], wc_ref[...], preferred_element_type=F32))
    o_ref[...] = x_ref[...] + jnp.dot(m.astype(BF16), wo_ref[...], preferred_element_type=F32)


def _merge_out(x, u, o_a, o_b, o_c, wa, wb, wc, wo):
    t, d = x.shape
    full = pl.BlockSpec((TM_MERGE, d), lambda i: (i, 0))
    mixo = pl.BlockSpec((TM_MERGE, MIX), lambda i: (i, 0))
    resident = lambda r, c: pl.BlockSpec((r, c), lambda i: (0, 0), pipeline_mode=pl.Buffered(1))
    return pl.pallas_call(
        _merge_kernel,
        grid=(t // TM_MERGE,),
        in_specs=[full, mixo, mixo, mixo, pl.BlockSpec((TM_MERGE, GATE_COLS), lambda i: (i, 0)),
                  resident(MIX, d), resident(MIX, d), resident(MIX, d), resident(d, d)],
        out_specs=full,
        out_shape=jax.ShapeDtypeStruct((t, d), F32),
        compiler_params=_cparams(1),
        name="merge_out_proj",
    )(x, o_a, o_b, o_c, u, wa, wb, wc, wo)


def _ffn_kernel(x_ref, g_ref, wz_ref, wv_ref, cw_ref, cb_ref, wd_ref, o_ref, h_ref, carry_ref, *,
                tiles_per_seq):
    i, j = pl.program_id(0), pl.program_id(1)

    @pl.when(j == 0)
    def _():
        x = x_ref[...]
        h_ref[...] = _rms_rows(x, g_ref[...]).astype(BF16)
        o_ref[...] = x

    @pl.when(jnp.logical_and(j == 0, i % tiles_per_seq == 0))
    def _():
        carry_ref[...] = jnp.zeros_like(carry_ref)

    h = h_ref[...]
    z = jnp.dot(h, wz_ref[...], preferred_element_type=F32)
    val = jnp.dot(h, wv_ref[...], preferred_element_type=F32)
    slot = pl.ds(pl.multiple_of(j * 8, 8), 8)
    tail = carry_ref[slot, :]
    carry_ref[slot, :] = z[TM_FF - 8:TM_FF, :]
    row = lax.broadcasted_iota(jnp.int32, (TM_FF, 1), 0)
    z1 = jnp.where(row == 0, tail[7:8, :], pltpu.roll(z, 1, 0))
    z2 = jnp.where(row == 0, tail[6:7, :], jnp.where(row == 1, tail[7:8, :], pltpu.roll(z, 2, 0)))
    cw = cw_ref[...]
    zc = z * cw[2:3, :] + z1 * cw[1:2, :] + z2 * cw[0:1, :] + cb_ref[...]
    gelu = 0.5 * zc * (1.0 + jnp.tanh(math.sqrt(2.0 / math.pi) * (zc + GELU_CUBIC * (zc * zc * zc))))
    o_ref[...] += jnp.dot((gelu * val).astype(BF16), wd_ref[...], preferred_element_type=F32)


def _ffn(x, gain, w_up, cw, cb, wd, seq):
    t, d = x.shape
    nj = FF_PAD // TN_FF
    full = pl.BlockSpec((TM_FF, d), lambda i, j: (i, 0))
    return pl.pallas_call(
        functools.partial(_ffn_kernel, tiles_per_seq=seq // TM_FF),
        grid=(t // TM_FF, nj),
        in_specs=[full, pl.BlockSpec((1, d), lambda i, j: (0, 0)),
                  pl.BlockSpec((d, TN_FF), lambda i, j: (0, j)),
                  pl.BlockSpec((d, TN_FF), lambda i, j: (0, nj + j)),
                  pl.BlockSpec((CONV_W, TN_FF), lambda i, j: (0, j)),
                  pl.BlockSpec((1, TN_FF), lambda i, j: (0, j)),
                  pl.BlockSpec((TN_FF, d), lambda i, j: (j, 0))],
        out_specs=full,
        out_shape=jax.ShapeDtypeStruct((t, d), F32),
        scratch_shapes=[pltpu.VMEM((TM_FF, d), BF16), pltpu.VMEM((nj * 8, TN_FF), F32)],
        compiler_params=_cparams(2),
        name="ffn_geglu_conv",
    )(x, gain.reshape(1, d), w_up, w_up, cw, cb.reshape(1, FF_PAD), wd)


def _ple_kernel(x_ref, g_ref, wg_ref, p_ref, wp_ref, fg_ref, o_ref, *, final):
    x = x_ref[...]
    h = _rms_rows(x, g_ref[...])
    gate = _sigmoid(_dot(h, wg_ref[...]))
    y = x + gate * _dot(p_ref[...], wp_ref[...])
    if final:
        y = _rms_rows(y, fg_ref[...])
    o_ref[...] = y


def _ple(x, gain, wg, p, wp, final_gain, final):
    t, d = x.shape
    full = pl.BlockSpec((TM, d), lambda i: (i, 0))
    vec = pl.BlockSpec((1, d), lambda i: (0, 0))
    return pl.pallas_call(
        functools.partial(_ple_kernel, final=final),
        grid=(t // TM,),
        in_specs=[full, vec, pl.BlockSpec((d, d), lambda i: (0, 0)),
                  pl.BlockSpec((TM, PLE_DIM), lambda i: (i, 0)),
                  pl.BlockSpec((PLE_DIM, d), lambda i: (0, 0)), vec],
        out_specs=full,
        out_shape=jax.ShapeDtypeStruct((t, d), F32),
        compiler_params=_cparams(1),
        name="ple_embed",
    )(x, gain.reshape(1, d), wg, p, wp, final_gain.reshape(1, d))


def _pad_cols(a, n):
    return jnp.pad(a, ((0, 0), (0, n - a.shape[1])))


def _pad_rows(a, n):
    return jnp.pad(a, ((0, n - a.shape[0]), (0, 0)))


def _pad_vec(a, n):
    return jnp.pad(a, (0, n - a.shape[0]))


def _layout_w_in(w):
    w = w.astype(BF16)
    o_rwkv = GLA_COLS
    o_ret = GLA_COLS + RWKV_COLS
    o_gate = o_ret + RET_COLS
    main = jnp.concatenate([w[:, o_gate:o_gate + GATE_COLS], w[:, 0:SEG],
                            w[:, o_rwkv:o_rwkv + SEG], w[:, o_ret:o_ret + SEG]], axis=1)
    o = o_rwkv + SEG
    small = jnp.concatenate([
        _pad_cols(w[:, o:o + RWKV_W_LORA], LANES),
        _pad_cols(w[:, o + RWKV_W_LORA:o + RWKV_W_LORA + RWKV_A_LORA], LANES),
        _pad_cols(w[:, o + RWKV_W_LORA + RWKV_A_LORA:o + RWKV_W_LORA + RWKV_A_LORA + RWKV_G_LORA], 2 * LANES),
        _pad_cols(w[:, SEG:SEG + GLA_LORA], LANES)], axis=1)
    return main, small


def _layout_mu(mu):
    o = SEG
    small = jnp.concatenate([
        _pad_vec(mu[o:o + RWKV_W_LORA], LANES),
        _pad_vec(mu[o + RWKV_W_LORA:o + RWKV_W_LORA + RWKV_A_LORA], LANES),
        _pad_vec(mu[o + RWKV_W_LORA + RWKV_A_LORA:], 2 * LANES)])
    return mu[:SEG], small


def kernel(x, p, positions, mix_norm, w_in, gla_w_decay, gla_b_decay, gla_norm, rwkv_mu, rwkv_w0, rwkv_w2, rwkv_a0, rwkv_a2, rwkv_g2, rwkv_k_k, rwkv_k_a, rwkv_r_k, rwkv_ln_w, rwkv_ln_b, rwkv_v0, rwkv_v1, rwkv_v2, w_br_gla, w_br_rwkv, w_br_ret, w_o, ffn_norm, w_up, conv_w, conv_b, w_down, ple_norm, w_ple_gate, w_ple_proj, final_norm):
    b, s, d = x.shape
    depth = w_in.shape[0]
    t = b * s
    assert d == D_MODEL and s % TM_FF == 0 and s % ROWS == 0 and t % TM_IN == 0
    xf = x.reshape(t, d)
    cos2, sin2 = _rope_tables(positions.astype(F32).reshape(t, 1))
    v_first = None
    for i in range(depth):
        w_main, w_small = _layout_w_in(w_in[i])
        u, small = _in_proj(xf, mix_norm[i], w_main, w_small)
        o_a = _gla(u, small, _pad_rows(gla_w_decay[i], LANES), gla_b_decay[i], gla_norm[i], s)
        mu_main, mu_small = _layout_mu(rwkv_mu[i])
        prm = dict(mu=mu_main, mu_s=mu_small, w0=rwkv_w0[i], w2=_pad_rows(rwkv_w2[i], LANES).astype(BF16),
                   a0=rwkv_a0[i], a2=_pad_rows(rwkv_a2[i], LANES).astype(BF16),
                   g2=_pad_rows(rwkv_g2[i], 2 * LANES).astype(BF16), k_k=rwkv_k_k[i], k_a=rwkv_k_a[i],
                   r_k=rwkv_r_k[i].reshape(MIX), ln_w=rwkv_ln_w[i], ln_b=rwkv_ln_b[i])
        if i > 0:
            prm.update(v0=rwkv_v0[i - 1], v1=_pad_cols(rwkv_v1[i - 1], LANES).astype(BF16),
                       v2=_pad_rows(rwkv_v2[i - 1], LANES).astype(BF16))
        o_b, v_first = _rwkv(u, small, prm, v_first, s)
        o_c = _retention(u, cos2, sin2, s)
        xf = _merge_out(xf, u, o_a, o_b, o_c, w_br_gla[i].astype(BF16), w_br_rwkv[i].astype(BF16),
                        w_br_ret[i].astype(BF16), w_o[i].astype(BF16))
        w_up_pad = jnp.pad(w_up[i].reshape(d, 2, D_FF), ((0, 0), (0, 0), (0, FF_PAD - D_FF)))
        xf = _ffn(xf, ffn_norm[i], w_up_pad.reshape(d, 2 * FF_PAD).astype(BF16),
                  _pad_cols(conv_w[i], FF_PAD), _pad_vec(conv_b[i], FF_PAD),
                  _pad_rows(w_down[i], FF_PAD).astype(BF16), s)
        xf = _ple(xf, ple_norm[i], w_ple_gate[i].astype(BF16), p[i].reshape(t, PLE_DIM),
                  w_ple_proj[i].astype(BF16), final_norm, final=(i == depth - 1))
    return xf.reshape(b, s, d)
```

```python
import functools
import math

import numpy as np
import jax
import jax.numpy as jnp
from jax import lax
from jax.experimental import pallas as pl
from jax.experimental.pallas import tpu as pltpu

F32 = jnp.float32
BF16 = jnp.bfloat16

D_MODEL = 2048
PLE_DIM = 256
MIX = D_MODEL // 2
CHUNK = 64
EPS = 1e-6
GLA_HEADS, GLA_DK, GLA_DV, GLA_LORA = 4, 128, 256, 16
GLA_GATE_NORMALIZER = 16.0
RWKV_HEAD = 64
RWKV_PAIRS = MIX // (2 * RWKV_HEAD)
RWKV_W_LORA, RWKV_A_LORA, RWKV_V_LORA, RWKV_G_LORA = 64, 64, 32, 160
RWKV_GN_EPS = 64e-5
RET_HEADS, RET_DK, RET_DV = 4, 128, 256
ROPE_BASE = 10000.0
D_FF = 5504
CONV_W = 3
GELU_CUBIC = 0.044715
GLA_QK = GLA_HEADS * GLA_DK
RET_QK = RET_HEADS * RET_DK
GLA_COLS = 2 * GLA_QK + 2 * MIX + GLA_LORA
RWKV_COLS = 3 * MIX + RWKV_W_LORA + RWKV_A_LORA + RWKV_G_LORA
RET_COLS = 2 * RET_QK + 2 * MIX
GATE_COLS = 3 * D_MODEL

LANES = 128
VMEM_LIMIT = 56 * 1024 * 1024

SEG = 3 * MIX
N_MAIN = GATE_COLS + 3 * SEG
SM_WL, SM_AL, SM_GL, SM_LR = 0, 128, 256, 512
N_SMALL = 640
FF_PAD = 5632

TM = 512
TM_IN = 512
TN_IN = 3072
TM_MERGE = 256
TM_FF = 1024
TN_FF = 512
ROWS = 256


def _cparams(n_axes):
    return pltpu.CompilerParams(dimension_semantics=("arbitrary",) * n_axes,
                                vmem_limit_bytes=VMEM_LIMIT)


def _dot(a, b):
    return jnp.dot(a.astype(BF16), b.astype(BF16), preferred_element_type=F32)


def _dot_nt(a, b):
    return lax.dot_general(a.astype(BF16), b.astype(BF16), (((1,), (1,)), ((), ())),
                           preferred_element_type=F32)


def _dot_tn(a, b):
    return lax.dot_general(a.astype(BF16), b.astype(BF16), (((0,), (0,)), ((), ())),
                           preferred_element_type=F32)


def _split(x, parts):
    out = []
    for _ in range(parts):
        p = x.astype(BF16)
        out.append(p)
        x = x - p.astype(F32)
    return out


def _dot_lhs_exact(m_bf16, x, parts):
    acc = None
    for p in _split(x, parts):
        t = jnp.dot(m_bf16, p, preferred_element_type=F32)
        acc = t if acc is None else acc + t
    return acc


def _sigmoid(x):
    return 1.0 / (1.0 + jnp.exp(-x))


def _silu(x):
    return x * _sigmoid(x)


def _softplus(x):
    return jnp.maximum(x, 0.0) + jnp.log(1.0 + jnp.exp(-jnp.abs(x)))


def _rms_rows(x, gain):
    return x * lax.rsqrt(jnp.mean(x * x, axis=-1, keepdims=True) + EPS) * gain


def _tri_incl(n):
    r = lax.broadcasted_iota(jnp.int32, (n, n), 0)
    c = lax.broadcasted_iota(jnp.int32, (n, n), 1)
    return r >= c


def _in_proj_kernel(x_ref, g_ref, w_ref, ws_ref, o_ref, os_ref, h_ref):
    @pl.when(pl.program_id(1) == 0)
    def _():
        h = _rms_rows(x_ref[...], g_ref[...]).astype(BF16)
        h_ref[...] = h
        os_ref[...] = jnp.dot(h, ws_ref[...], preferred_element_type=F32)

    o_ref[...] = jnp.dot(h_ref[...], w_ref[...], preferred_element_type=F32).astype(o_ref.dtype)


def _in_proj(x, gain, w_main, w_small):
    t, d = x.shape
    return pl.pallas_call(
        _in_proj_kernel,
        grid=(t // TM_IN, N_MAIN // TN_IN),
        in_specs=[pl.BlockSpec((TM_IN, d), lambda i, j: (i, 0)),
                  pl.BlockSpec((1, d), lambda i, j: (0, 0)),
                  pl.BlockSpec((d, TN_IN), lambda i, j: (0, j)),
                  pl.BlockSpec((d, N_SMALL), lambda i, j: (0, 0))],
        out_specs=[pl.BlockSpec((TM_IN, TN_IN), lambda i, j: (i, j)),
                   pl.BlockSpec((TM_IN, N_SMALL), lambda i, j: (i, 0))],
        out_shape=[jax.ShapeDtypeStruct((t, N_MAIN), BF16), jax.ShapeDtypeStruct((t, N_SMALL), F32)],
        scratch_shapes=[pltpu.VMEM((TM_IN, d), BF16)],
        compiler_params=_cparams(2),
        name="in_proj",
    )(x, gain.reshape(1, d), w_main, w_small)


def _rope_kernel(pos_ref, f_ref, s_ref, cos_ref, sin_ref):
    ang = pos_ref[...] * f_ref[...]
    cos_ref[...] = jnp.cos(ang)
    sin_ref[...] = jnp.sin(ang) * s_ref[...]


def _rope_tables(pos):
    t = pos.shape[0]
    half = RET_DK // 2
    inv = (ROPE_BASE ** (-np.arange(0, RET_DK, 2, dtype=np.float32) / np.float32(RET_DK))).astype(np.float32)
    inv2 = jnp.asarray(np.concatenate([inv, inv]).reshape(1, RET_DK))
    sign = jnp.asarray(np.concatenate([-np.ones(half, np.float32), np.ones(half, np.float32)]).reshape(1, RET_DK))
    rows = TM
    return pl.pallas_call(
        _rope_kernel,
        grid=(t // rows,),
        in_specs=[pl.BlockSpec((rows, 1), lambda i: (i, 0)),
                  pl.BlockSpec((1, RET_DK), lambda i: (0, 0)),
                  pl.BlockSpec((1, RET_DK), lambda i: (0, 0))],
        out_specs=[pl.BlockSpec((rows, RET_DK), lambda i: (i, 0))] * 2,
        out_shape=[jax.ShapeDtypeStruct((t, RET_DK), F32)] * 2,
        compiler_params=_cparams(1),
        name="rope_tables",
    )(pos, inv2, sign)


def _gla_kernel(u_ref, lr_ref, wdh_ref, wdl_ref, bd_ref, ng_ref, o_ref, st_ref, la_ref, *, steps_per_seq):
    @pl.when(pl.program_id(0) % steps_per_seq == 0)
    def _():
        st_ref[...] = jnp.zeros_like(st_ref)

    lr_hi, lr_lo = _split(lr_ref[...], 2)
    z = (jnp.dot(lr_hi, wdh_ref[...], preferred_element_type=F32)
         + (jnp.dot(lr_lo, wdh_ref[...], preferred_element_type=F32)
            + jnp.dot(lr_hi, wdl_ref[...], preferred_element_type=F32))) + bd_ref[...]
    la_ref[...] = -_softplus(-z) * (1.0 / GLA_GATE_NORMALIZER)

    causal = _tri_incl(CHUNK)
    tri = causal.astype(BF16)
    scale = GLA_DK ** -0.5
    ng = ng_ref[...]

    heads = range(GLA_HEADS)
    kcol = [slice(h * GLA_DK, (h + 1) * GLA_DK) for h in heads]
    vcol = [slice(h * GLA_DV, (h + 1) * GLA_DV) for h in heads]

    chunks = range(ROWS // CHUNK)
    units = [(c, h) for c in chunks for h in heads]
    rows = [slice(c * CHUNK, (c + 1) * CHUNK) for c in chunks]
    seg = lambda off, col: [u_ref[rows[c], off + col[h].start:off + col[h].stop] for c, h in units]
    q, k = seg(0, kcol), seg(GLA_QK, kcol)
    vb, g = seg(2 * GLA_QK, vcol), seg(2 * GLA_QK + MIX, vcol)
    cum_c = [_dot_lhs_exact(tri, la_ref[rows[c], :], 2) for c in chunks]
    cum = [cum_c[c][:, kcol[h]] for c, h in units]
    n = range(len(units))
    cl = [cum[i][CHUNK - 1:CHUNK, :] for i in n]
    q_in = [(q[i] * jnp.exp(cum[i]) * scale).astype(BF16) for i in n]
    k_in = [k[i] * jnp.exp(-cum[i]) for i in n]
    k_st = [k[i] * jnp.exp(cl[i] - cum[i]) for i in n]
    sc = [jnp.where(causal, _dot_nt(q_in[i], k_in[i]), 0.0) for i in n]
    kv = [_dot_tn(vb[i], k_st[i]) for i in n]
    o = [_dot(sc[i], vb[i]) for i in n]
    st = [None] * len(units)
    for h in heads:
        cur = st_ref[h]
        for c in chunks:
            i = c * GLA_HEADS + h
            st[i] = cur
            cur = cur * jnp.exp(cl[i]) + kv[i]
        st_ref[h] = cur
    o = [o[i] + _dot_nt(q_in[i], st[i]) for i in n]
    for i, (c, h) in enumerate(units):
        y = o[i] * lax.rsqrt(jnp.mean(o[i] * o[i], axis=-1, keepdims=True) + EPS) * ng
        o_ref[rows[c], vcol[h]] = (y * _silu(g[i].astype(F32))).astype(o_ref.dtype)


def _gla(u, small, wd_pad, b_decay, norm_g, seq):
    t = u.shape[0]
    wd_hi = wd_pad.astype(BF16)
    wd_hi_residual = (wd_pad - wd_hi.astype(F32)).astype(BF16)
    return pl.pallas_call(
        functools.partial(_gla_kernel, steps_per_seq=seq // ROWS),
        grid=(t // ROWS,),
        in_specs=[pl.BlockSpec((ROWS, SEG), lambda i: (i, GATE_COLS // SEG)),
                  pl.BlockSpec((ROWS, LANES), lambda i: (i, SM_LR // LANES)),
                  pl.BlockSpec((LANES, GLA_QK), lambda i: (0, 0)),
                  pl.BlockSpec((LANES, GLA_QK), lambda i: (0, 0)),
                  pl.BlockSpec((1, GLA_QK), lambda i: (0, 0)),
                  pl.BlockSpec((1, GLA_DV), lambda i: (0, 0))],
        out_specs=pl.BlockSpec((ROWS, MIX), lambda i: (i, 0)),
        out_shape=jax.ShapeDtypeStruct((t, MIX), BF16),
        scratch_shapes=[pltpu.VMEM((GLA_HEADS, GLA_DV, GLA_DK), F32),
                        pltpu.VMEM((ROWS, GLA_QK), F32)],
        compiler_params=_cparams(1),
        name="gla_mixer",
    )(u, small, wd_hi, wd_hi_residual, b_decay.reshape(1, GLA_QK), norm_g.reshape(1, GLA_DV))


def _ret_consts():
    lg = np.log(1.0 - np.exp2(-5.0 - np.arange(RET_HEADS, dtype=np.float32))).astype(np.float32)
    pos = np.arange(CHUNK, dtype=np.float32)
    diff = pos[:, None] - pos[None, :]
    causal = diff >= 0
    dmask = np.where(causal, np.exp(lg[:, None, None] * np.where(causal, diff, 0.0)), 0.0).astype(np.float32)
    qs = np.exp(lg[:, None] * (pos[None, :] + 1.0)).astype(np.float32)
    vs = np.exp(lg[:, None] * (CHUNK - 1.0 - pos[None, :])).astype(np.float32)
    cd = [float(np.exp(np.float32(l) * np.float32(CHUNK))) for l in lg]
    qs_b = np.broadcast_to(qs[:, :, None], (RET_HEADS, CHUNK, RET_DK)).copy()
    vs_b = np.broadcast_to(vs[:, :, None], (RET_HEADS, CHUNK, RET_DK)).copy()
    return dmask, qs_b, vs_b, cd


def _ret_kernel(u_ref, cos_ref, sin_ref, dm_ref, qs_ref, vs_ref, o_ref, st_ref, *, steps_per_seq, chunk_decay):
    @pl.when(pl.program_id(0) % steps_per_seq == 0)
    def _():
        st_ref[...] = jnp.zeros_like(st_ref)

    scale = RET_DK ** -0.5
    half = RET_DK // 2

    heads = range(RET_HEADS)
    kcol = [slice(h * RET_DK, (h + 1) * RET_DK) for h in heads]
    vcol = [slice(h * RET_DV, (h + 1) * RET_DV) for h in heads]

    chunks = range(ROWS // CHUNK)
    units = [(c, h) for c in chunks for h in heads]
    rows = [slice(c * CHUNK, (c + 1) * CHUNK) for c in chunks]
    cos = [cos_ref[rows[c], :] for c in chunks]
    sin = [sin_ref[rows[c], :] for c in chunks]
    seg = lambda off, col: [u_ref[rows[c], off + col[h].start:off + col[h].stop] for c, h in units]
    q, k = seg(0, kcol), seg(RET_QK, kcol)
    vb, g = seg(2 * RET_QK, vcol), seg(2 * RET_QK + MIX, vcol)
    n = range(len(units))
    rot = lambda x, c: x * cos[c] + pltpu.roll(x, half, 1) * sin[c]
    q = [rot(q[i].astype(F32), c) * scale for i, (c, h) in enumerate(units)]
    k = [rot(k[i].astype(F32), c) for i, (c, h) in enumerate(units)]
    sc = [_dot_nt(q[i], k[i]) * dm_ref[h] for i, (c, h) in enumerate(units)]
    kv = [_dot_tn(vb[i], k[i] * vs_ref[h]) for i, (c, h) in enumerate(units)]
    o = [_dot(sc[i], vb[i]) for i in n]
    st = [None] * len(units)
    for h in heads:
        cur = st_ref[h]
        for c in chunks:
            i = c * RET_HEADS + h
            st[i] = cur
            cur = cur * chunk_decay[h] + kv[i]
        st_ref[h] = cur
    o = [o[i] + _dot_nt(q[i] * qs_ref[h], st[i]) for i, (c, h) in enumerate(units)]
    for i, (c, h) in enumerate(units):
        y = o[i] * lax.rsqrt(jnp.mean(o[i] * o[i], axis=-1, keepdims=True) + EPS)
        o_ref[rows[c], vcol[h]] = (_silu(g[i].astype(F32)) * y).astype(o_ref.dtype)


def _retention(u, cos2, sin2, seq):
    t = u.shape[0]
    dmask, qs_b, vs_b, cd = _ret_consts()
    return pl.pallas_call(
        functools.partial(_ret_kernel, steps_per_seq=seq // ROWS, chunk_decay=tuple(cd)),
        grid=(t // ROWS,),
        in_specs=[pl.BlockSpec((ROWS, SEG), lambda i: (i, GATE_COLS // SEG + 2)),
                  pl.BlockSpec((ROWS, RET_DK), lambda i: (i, 0)),
                  pl.BlockSpec((ROWS, RET_DK), lambda i: (i, 0)),
                  pl.BlockSpec((RET_HEADS, CHUNK, CHUNK), lambda i: (0, 0, 0)),
                  pl.BlockSpec((RET_HEADS, CHUNK, RET_DK), lambda i: (0, 0, 0)),
                  pl.BlockSpec((RET_HEADS, CHUNK, RET_DK), lambda i: (0, 0, 0))],
        out_specs=pl.BlockSpec((ROWS, MIX), lambda i: (i, 0)),
        out_shape=jax.ShapeDtypeStruct((t, MIX), BF16),
        scratch_shapes=[pltpu.VMEM((RET_HEADS, RET_DV, RET_DK), F32)],
        compiler_params=_cparams(1),
        name="retention_mixer",
    )(u, cos2, sin2, jnp.asarray(dmask), jnp.asarray(qs_b), jnp.asarray(vs_b))


def _rwkv_kernel(*refs, steps_per_seq, first_layer):
    if first_layer:
        (u_ref, sm_ref, mu_ref, mus_ref, w0_ref, w2_ref, a0_ref, a2_ref, g2_ref, kk_ref, ka_ref,
         rk_ref, lnw_ref, lnb_ref, o_ref, vf_out_ref,
         st_ref, cu_ref, cs_ref, r_s, lw_s, k_s, v_s, kk_s, a_s, g_s) = refs
    else:
        (u_ref, sm_ref, mu_ref, mus_ref, w0_ref, w2_ref, a0_ref, a2_ref, g2_ref, kk_ref, ka_ref,
         rk_ref, lnw_ref, lnb_ref, vf_ref, v0_ref, v1_ref, v2_ref, o_ref,
         st_ref, cu_ref, cs_ref, r_s, lw_s, k_s, v_s, kk_s, a_s, g_s) = refs

    @pl.when(pl.program_id(0) % steps_per_seq == 0)
    def _():
        st_ref[...] = jnp.zeros_like(st_ref)
        cu_ref[...] = jnp.zeros_like(cu_ref)
        cs_ref[...] = jnp.zeros_like(cs_ref)

    row = lax.broadcasted_iota(jnp.int32, (ROWS, 1), 0)

    def shifted(x, carry_ref, mu):
        prev = jnp.where(row == 0, carry_ref[0:1, :], pltpu.roll(x, 1, 0))
        carry_ref[0:1, :] = x[ROWS - 1:ROWS, :]
        return x + (prev - x) * mu

    us = shifted(u_ref[...].astype(F32), cu_ref, mu_ref[...])
    ss = shifted(sm_ref[...], cs_ref, mus_ref[...])
    r = us[:, 0:MIX]
    k = us[:, MIX:2 * MIX]
    v = us[:, 2 * MIX:3 * MIX]
    wl = ss[:, SM_WL:SM_WL + LANES]
    al = ss[:, SM_AL:SM_AL + LANES]
    gl = ss[:, SM_GL:SM_GL + 2 * LANES]
    w = w0_ref[...] + _dot(jnp.tanh(wl), w2_ref[...])
    lw_s[...] = -math.exp(-0.5) * _sigmoid(w)
    a = _sigmoid(a0_ref[...] + _dot(al, a2_ref[...]))
    g_s[...] = _dot(_sigmoid(gl), g2_ref[...])
    if first_layer:
        vf_out_ref[...] = v
    else:
        mix = _sigmoid(v0_ref[...] + _dot(_dot(v, v1_ref[...]), v2_ref[...]))
        v = v + (vf_ref[...] - v) * mix
    r_s[...] = r
    v_s[...] = v
    a_s[...] = a
    kk_s[...] = k * kk_ref[...]
    k_s[...] = k * (1.0 + (a - 1.0) * ka_ref[...])

    lane = lax.broadcasted_iota(jnp.int32, (1, LANES), 1)
    mlo = (lane < RWKV_HEAD).astype(F32)
    mhi = 1.0 - mlo
    rr = lax.broadcasted_iota(jnp.int32, (LANES, LANES), 0)
    cc = lax.broadcasted_iota(jnp.int32, (LANES, LANES), 1)
    bd_mask = (rr // RWKV_HEAD) == (cc // RWKV_HEAD)
    bd = bd_mask.astype(BF16)
    eye = (rr == cc).astype(F32)
    tri = _tri_incl(CHUNK).astype(BF16)
    trow = lax.broadcasted_iota(jnp.int32, (CHUNK, LANES), 0)
    scol = lax.broadcasted_iota(jnp.int32, (CHUNK, LANES), 1) % RWKV_HEAD
    strict = scol < trow
    incl = scol <= trow
    inv_n = 1.0 / RWKV_HEAD

    def stack(x):
        return jnp.concatenate([x * mlo, x * mhi], axis=0)

    pairs = range(RWKV_PAIRS)
    cols = [slice(p * LANES, (p + 1) * LANES) for p in pairs]
    n_double = int(math.log2(CHUNK)) - 1

    def head_sums(tiles):
        s_all = jnp.dot(jnp.concatenate(tiles, axis=0).astype(BF16), bd, preferred_element_type=F32)
        return [s_all[p * CHUNK:(p + 1) * CHUNK] for p in range(len(tiles))]

    def prepare(c, ctx):
        rows = slice(c * CHUNK, (c + 1) * CHUNK)
        tile = lambda ref: [ref[rows, cols[p]] for p in pairs]
        r_t, lw_t, k_t, v_t, kkr, a_t = map(tile, (r_s, lw_s, k_s, v_s, kk_s, a_s))
        nrm = head_sums([kkr[p] * kkr[p] for p in pairs])
        cum_all = _dot_lhs_exact(tri, lw_s[rows, :], 2)
        yield
        cum = [cum_all[:, cols[p]] for p in pairs]
        kk = [kkr[p] * lax.rsqrt(jnp.maximum(nrm[p], 1e-24)) for p in pairs]
        bv = [kk[p] * a_t[p] for p in pairs]
        cl = [cum[p][CHUNK - 1:CHUNK, :] for p in pairs]
        e_out = [jnp.exp(-cum[p]) for p in pairs]
        e_end = [jnp.exp(cl[p] - cum[p]) for p in pairs]
        at = [-kk[p] * jnp.exp(cum[p] - lw_t[p]) for p in pairs]
        rt = [r_t[p] * jnp.exp(cum[p]) for p in pairs]
        gm = [_dot_nt(jnp.concatenate([at[p], rt[p]], axis=0),
                      jnp.concatenate([stack(bv[p] * e_out[p]), stack(k_t[p] * e_out[p])], axis=0))
              for p in pairs]
        yield
        ab = [jnp.where(strict, gm[p][0:CHUNK, 0:LANES], 0.0) for p in pairs]
        ak = [jnp.where(strict, gm[p][0:CHUNK, LANES:2 * LANES], 0.0) for p in pairs]
        rb = [jnp.where(incl, gm[p][CHUNK:2 * CHUNK, 0:LANES], 0.0) for p in pairs]
        rk = [jnp.where(incl, gm[p][CHUNK:2 * CHUNK, LANES:2 * LANES], 0.0) for p in pairs]
        vs2 = [stack(v_t[p]) for p in pairs]
        npow = [stack(ab[p]) for p in pairs]
        tinv = [eye + npow[p] for p in pairs]
        npow = [_dot(npow[p], npow[p]) for p in pairs]
        akv = [_dot(ak[p], vs2[p]) for p in pairs]
        rkv = [_dot(rk[p], vs2[p]) for p in pairs]
        yield
        for _ in range(n_double - 1):
            prod = [_dot(tinv[p], npow[p]) for p in pairs]
            npow = [_dot(npow[p], npow[p]) for p in pairs]
            yield
            tinv = [tinv[p] + prod[p] for p in pairs]
        prod = [_dot(tinv[p], npow[p]) for p in pairs]
        rkk = head_sums([r_t[p] * k_t[p] * rk_ref[:, cols[p]] for p in pairs])
        yield
        tinv = [tinv[p] + prod[p] for p in pairs]
        ctx.update(
            tcat=[tinv[p][0:CHUNK] + tinv[p][CHUNK:2 * CHUNK] for p in pairs],
            kb_end=[jnp.concatenate([bv[p] * e_end[p], k_t[p] * e_end[p]], axis=0) for p in pairs],
            bonus=[rkk[p] * v_t[p] for p in pairs],
            at=at, rt=rt, akv=akv, rkv=rkv, rb=rb, cl=cl, v_t=v_t)

    def apply(c, ctx):
        rows = slice(c * CHUNK, (c + 1) * CHUNK)
        st = [st_ref[p] for p in pairs]
        u_rhs = [_dot_nt(ctx["at"][p], st[p]) + ctx["akv"][p] for p in pairs]
        y_st = [_dot_nt(ctx["rt"][p], st[p]) for p in pairs]
        yield
        u_t = [_dot(ctx["tcat"][p], stack(u_rhs[p])) for p in pairs]
        yield
        y = [y_st[p] + _dot(ctx["rb"][p], stack(u_t[p])) + ctx["rkv"][p] for p in pairs]
        upd = [_dot_tn(jnp.concatenate([u_t[p], ctx["v_t"][p]], axis=0), ctx["kb_end"][p]) for p in pairs]
        yield
        for p in pairs:
            st_ref[p] = st[p] * jnp.exp(ctx["cl"][p]) + jnp.where(bd_mask, upd[p], 0.0)
        mean = head_sums(y)
        yield
        dev = [y[p] - mean[p] * inv_n for p in pairs]
        var = head_sums([dev[p] * dev[p] for p in pairs])
        yield
        for p in pairs:
            yn = dev[p] * lax.rsqrt(var[p] * inv_n + RWKV_GN_EPS) * lnw_ref[:, cols[p]] + lnb_ref[:, cols[p]]
            o_ref[rows, cols[p]] = ((yn + ctx["bonus"][p]) * g_s[rows, cols[p]]).astype(o_ref.dtype)

    def run_interleaved(*gens):
        live = list(gens)
        while live:
            for g in list(live):
                if next(g, StopIteration) is StopIteration:
                    live.remove(g)

    n_chunks = ROWS // CHUNK
    ctx = {}
    run_interleaved(prepare(0, ctx))
    for c in range(n_chunks):
        nxt = {}
        if c + 1 < n_chunks:
            run_interleaved(apply(c, ctx), prepare(c + 1, nxt))
        else:
            run_interleaved(apply(c, ctx))
        ctx = nxt


def _rwkv(u, small, prm, v_first, seq):
    t = u.shape[0]
    first = v_first is None
    row = lambda a: a.reshape(1, -1)
    vec = lambda n: pl.BlockSpec((1, n), lambda i: (0, 0))
    mat = lambda r, c: pl.BlockSpec((r, c), lambda i: (0, 0))
    args = [u, small, row(prm["mu"]), row(prm["mu_s"]), row(prm["w0"]), prm["w2"], row(prm["a0"]), prm["a2"],
            prm["g2"], row(prm["k_k"]), row(prm["k_a"]), row(prm["r_k"]), row(prm["ln_w"]), row(prm["ln_b"])]
    in_specs = [pl.BlockSpec((ROWS, SEG), lambda i: (i, GATE_COLS // SEG + 1)),
                pl.BlockSpec((ROWS, 4 * LANES), lambda i: (i, 0)),
                vec(SEG), vec(4 * LANES), vec(MIX), mat(LANES, MIX), vec(MIX), mat(LANES, MIX),
                mat(2 * LANES, MIX), vec(MIX), vec(MIX), vec(MIX), vec(MIX), vec(MIX)]
    out_block = pl.BlockSpec((ROWS, MIX), lambda i: (i, 0))
    if first:
        out_specs = [out_block, out_block]
        out_shape = [jax.ShapeDtypeStruct((t, MIX), BF16), jax.ShapeDtypeStruct((t, MIX), F32)]
    else:
        args += [v_first, row(prm["v0"]), prm["v1"], prm["v2"]]
        in_specs += [out_block, vec(MIX), mat(MIX, LANES), mat(LANES, MIX)]
        out_specs = out_block
        out_shape = jax.ShapeDtypeStruct((t, MIX), BF16)
    tok = pltpu.VMEM((ROWS, MIX), F32)
    res = pl.pallas_call(
        functools.partial(_rwkv_kernel, steps_per_seq=seq // ROWS, first_layer=first),
        grid=(t // ROWS,),
        in_specs=in_specs,
        out_specs=out_specs,
        out_shape=out_shape,
        scratch_shapes=[pltpu.VMEM((RWKV_PAIRS, LANES, LANES), F32),
                        pltpu.VMEM((8, SEG), F32), pltpu.VMEM((8, 4 * LANES), F32),
                        tok, tok, tok, tok, tok, tok, tok],
        compiler_params=_cparams(1),
        name="rwkv7_mixer",
    )(*args)
    return (res[0], res[1]) if first else (res, v_first)


def _merge_kernel(x_ref, oa_ref, ob_ref, oc_ref, g_ref, wa_ref, wb_ref, wc_ref, wo_ref, o_ref):
    d = D_MODEL
    gate = lambda k: _sigmoid(g_ref[:, k * d:(k + 1) * d].astype(F32))
    m = (gate(0) * jnp.dot(oa_ref[...], wa_ref[...], preferred_element_type=F32)
         + gate(1) * jnp.dot(ob_ref[...], wb_ref[...], preferred_element_type=F32)
         + gate(2) * jnp.dot(oc_ref[...], wc_ref[...], preferred_element_type=F32))
    o_ref[...] = x_ref[...] + jnp.dot(m.astype(BF16), wo_ref[...], preferred_element_type=F32)


def _merge_out(x, u, o_a, o_b, o_c, wa, wb, wc, wo):
    t, d = x.shape
    full = pl.BlockSpec((TM_MERGE, d), lambda i: (i, 0))
    mixo = pl.BlockSpec((TM_MERGE, MIX), lambda i: (i, 0))
    resident = lambda r, c: pl.BlockSpec((r, c), lambda i: (0, 0), pipeline_mode=pl.Buffered(1))
    return pl.pallas_call(
        _merge_kernel,
        grid=(t // TM_MERGE,),
        in_specs=[full, mixo, mixo, mixo, pl.BlockSpec((TM_MERGE, GATE_COLS), lambda i: (i, 0)),
                  resident(MIX, d), resident(MIX, d), resident(MIX, d), resident(d, d)],
        out_specs=full,
        out_shape=jax.ShapeDtypeStruct((t, d), F32),
        compiler_params=_cparams(1),
        name="merge_out_proj",
    )(x, o_a, o_b, o_c, u, wa, wb, wc, wo)


def _ffn_kernel(x_ref, g_ref, wz_ref, wv_ref, cw_ref, cb_ref, wd_ref, o_ref, h_ref, carry_ref, *,
                tiles_per_seq):
    i, j = pl.program_id(0), pl.program_id(1)

    @pl.when(j == 0)
    def _():
        x = x_ref[...]
        h_ref[...] = _rms_rows(x, g_ref[...]).astype(BF16)
        o_ref[...] = x

    @pl.when(jnp.logical_and(j == 0, i % tiles_per_seq == 0))
    def _():
        carry_ref[...] = jnp.zeros_like(carry_ref)

    h = h_ref[...]
    z = jnp.dot(h, wz_ref[...], preferred_element_type=F32)
    val = jnp.dot(h, wv_ref[...], preferred_element_type=F32)
    slot = pl.ds(pl.multiple_of(j * 8, 8), 8)
    tail = carry_ref[slot, :]
    carry_ref[slot, :] = z[TM_FF - 8:TM_FF, :]
    row = lax.broadcasted_iota(jnp.int32, (TM_FF, 1), 0)
    z1 = jnp.where(row == 0, tail[7:8, :], pltpu.roll(z, 1, 0))
    z2 = jnp.where(row == 0, tail[6:7, :], jnp.where(row == 1, tail[7:8, :], pltpu.roll(z, 2, 0)))
    cw = cw_ref[...]
    zc = z * cw[2:3, :] + z1 * cw[1:2, :] + z2 * cw[0:1, :] + cb_ref[...]
    gelu = 0.5 * zc * (1.0 + jnp.tanh(math.sqrt(2.0 / math.pi) * (zc + GELU_CUBIC * (zc * zc * zc))))
    o_ref[...] += jnp.dot((gelu * val).astype(BF16), wd_ref[...], preferred_element_type=F32)


def _ffn(x, gain, wz, wv, cw, cb, wd, seq):
    t, d = x.shape
    nj = FF_PAD // TN_FF
    full = pl.BlockSpec((TM_FF, d), lambda i, j: (i, 0))
    return pl.pallas_call(
        functools.partial(_ffn_kernel, tiles_per_seq=seq // TM_FF),
        grid=(t // TM_FF, nj),
        in_specs=[full, pl.BlockSpec((1, d), lambda i, j: (0, 0)),
                  pl.BlockSpec((d, TN_FF), lambda i, j: (0, j)),
                  pl.BlockSpec((d, TN_FF), lambda i, j: (0, j)),
                  pl.BlockSpec((CONV_W, TN_FF), lambda i, j: (0, j)),
                  pl.BlockSpec((1, TN_FF), lambda i, j: (0, j)),
                  pl.BlockSpec((TN_FF, d), lambda i, j: (j, 0))],
        out_specs=full,
        out_shape=jax.ShapeDtypeStruct((t, d), F32),
        scratch_shapes=[pltpu.VMEM((TM_FF, d), BF16), pltpu.VMEM((nj * 8, TN_FF), F32)],
        compiler_params=_cparams(2),
        name="ffn_geglu_conv",
    )(x, gain.reshape(1, d), wz, wv, cw, cb.reshape(1, FF_PAD), wd)


def _ple_kernel(x_ref, g_ref, wg_ref, p_ref, wp_ref, fg_ref, o_ref, *, final):
    x = x_ref[...]
    h = _rms_rows(x, g_ref[...])
    gate = _sigmoid(_dot(h, wg_ref[...]))
    y = x + gate * _dot(p_ref[...], wp_ref[...])
    if final:
        y = _rms_rows(y, fg_ref[...])
    o_ref[...] = y


def _ple(x, gain, wg, p, wp, final_gain, final):
    t, d = x.shape
    full = pl.BlockSpec((TM, d), lambda i: (i, 0))
    vec = pl.BlockSpec((1, d), lambda i: (0, 0))
    return pl.pallas_call(
        functools.partial(_ple_kernel, final=final),
        grid=(t // TM,),
        in_specs=[full, vec, pl.BlockSpec((d, d), lambda i: (0, 0)),
                  pl.BlockSpec((TM, PLE_DIM), lambda i: (i, 0)),
                  pl.BlockSpec((PLE_DIM, d), lambda i: (0, 0)), vec],
        out_specs=full,
        out_shape=jax.ShapeDtypeStruct((t, d), F32),
        compiler_params=_cparams(1),
        name="ple_embed",
    )(x, gain.reshape(1, d), wg, p, wp, final_gain.reshape(1, d))


def _pad_cols(a, n):
    return jnp.pad(a, ((0, 0), (0, n - a.shape[1])))


def _pad_rows(a, n):
    return jnp.pad(a, ((0, n - a.shape[0]), (0, 0)))


def _pad_vec(a, n):
    return jnp.pad(a, (0, n - a.shape[0]))


def _layout_w_in(w):
    o_rwkv = GLA_COLS
    o_ret = GLA_COLS + RWKV_COLS
    o_gate = o_ret + RET_COLS
    main = jnp.concatenate([w[:, o_gate:o_gate + GATE_COLS], w[:, 0:SEG],
                            w[:, o_rwkv:o_rwkv + SEG], w[:, o_ret:o_ret + SEG]], axis=1)
    o = o_rwkv + SEG
    small = jnp.concatenate([
        _pad_cols(w[:, o:o + RWKV_W_LORA], LANES),
        _pad_cols(w[:, o + RWKV_W_LORA:o + RWKV_W_LORA + RWKV_A_LORA], LANES),
        _pad_cols(w[:, o + RWKV_W_LORA + RWKV_A_LORA:o + RWKV_W_LORA + RWKV_A_LORA + RWKV_G_LORA], 2 * LANES),
        _pad_cols(w[:, SEG:SEG + GLA_LORA], LANES)], axis=1)
    return main.astype(BF16), small.astype(BF16)


def _layout_mu(mu):
    o = SEG
    small = jnp.concatenate([
        _pad_vec(mu[o:o + RWKV_W_LORA], LANES),
        _pad_vec(mu[o + RWKV_W_LORA:o + RWKV_W_LORA + RWKV_A_LORA], LANES),
        _pad_vec(mu[o + RWKV_W_LORA + RWKV_A_LORA:], 2 * LANES)])
    return mu[:SEG], small


def kernel(x, p, positions, mix_norm, w_in, gla_w_decay, gla_b_decay, gla_norm, rwkv_mu, rwkv_w0, rwkv_w2, rwkv_a0, rwkv_a2, rwkv_g2, rwkv_k_k, rwkv_k_a, rwkv_r_k, rwkv_ln_w, rwkv_ln_b, rwkv_v0, rwkv_v1, rwkv_v2, w_br_gla, w_br_rwkv, w_br_ret, w_o, ffn_norm, w_up, conv_w, conv_b, w_down, ple_norm, w_ple_gate, w_ple_proj, final_norm):
    b, s, d = x.shape
    depth = w_in.shape[0]
    t = b * s
    assert d == D_MODEL and s % TM_FF == 0 and s % ROWS == 0 and t % TM_IN == 0
    xf = x.reshape(t, d)
    cos2, sin2 = _rope_tables(positions.astype(F32).reshape(t, 1))
    v_first = None
    for i in range(depth):
        w_main, w_small = _layout_w_in(w_in[i])
        u, small = _in_proj(xf, mix_norm[i], w_main, w_small)
        o_a = _gla(u, small, _pad_rows(gla_w_decay[i], LANES), gla_b_decay[i], gla_norm[i], s)
        mu_main, mu_small = _layout_mu(rwkv_mu[i])
        prm = dict(mu=mu_main, mu_s=mu_small, w0=rwkv_w0[i], w2=_pad_rows(rwkv_w2[i], LANES).astype(BF16),
                   a0=rwkv_a0[i], a2=_pad_rows(rwkv_a2[i], LANES).astype(BF16),
                   g2=_pad_rows(rwkv_g2[i], 2 * LANES).astype(BF16), k_k=rwkv_k_k[i], k_a=rwkv_k_a[i],
                   r_k=rwkv_r_k[i].reshape(MIX), ln_w=rwkv_ln_w[i], ln_b=rwkv_ln_b[i])
        if i > 0:
            prm.update(v0=rwkv_v0[i - 1], v1=_pad_cols(rwkv_v1[i - 1], LANES).astype(BF16),
                       v2=_pad_rows(rwkv_v2[i - 1], LANES).astype(BF16))
        o_b, v_first = _rwkv(u, small, prm, v_first, s)
        o_c = _retention(u, cos2, sin2, s)
        xf = _merge_out(xf, u, o_a, o_b, o_c, w_br_gla[i].astype(BF16), w_br_rwkv[i].astype(BF16),
                        w_br_ret[i].astype(BF16), w_o[i].astype(BF16))
        wz = _pad_cols(w_up[i][:, :D_FF], FF_PAD).astype(BF16)
        wv = _pad_cols(w_up[i][:, D_FF:], FF_PAD).astype(BF16)
        xf = _ffn(xf, ffn_norm[i], wz, wv, _pad_cols(conv_w[i], FF_PAD), _pad_vec(conv_b[i], FF_PAD),
                  _pad_rows(w_down[i], FF_PAD).astype(BF16), s)
        xf = _ple(xf, ple_norm[i], w_ple_gate[i].astype(BF16), p[i].reshape(t, PLE_DIM),
                  w_ple_proj[i].astype(BF16), final_norm, final=(i == depth - 1))
    return xf.reshape(b, s, d)
```

```python
import functools
import math

import numpy as np
import jax
import jax.numpy as jnp
from jax import lax
from jax.experimental import pallas as pl
from jax.experimental.pallas import tpu as pltpu

F32 = jnp.float32
BF16 = jnp.bfloat16

D_MODEL = 2048
PLE_DIM = 256
MIX = D_MODEL // 2
CHUNK = 64
EPS = 1e-6
GLA_HEADS, GLA_DK, GLA_DV, GLA_LORA = 4, 128, 256, 16
GLA_GATE_NORMALIZER = 16.0
RWKV_HEAD = 64
RWKV_PAIRS = MIX // (2 * RWKV_HEAD)
RWKV_W_LORA, RWKV_A_LORA, RWKV_V_LORA, RWKV_G_LORA = 64, 64, 32, 160
RWKV_GN_EPS = 64e-5
RET_HEADS, RET_DK, RET_DV = 4, 128, 256
ROPE_BASE = 10000.0
D_FF = 5504
CONV_W = 3
GELU_CUBIC = 0.044715
GLA_QK = GLA_HEADS * GLA_DK
RET_QK = RET_HEADS * RET_DK
GLA_COLS = 2 * GLA_QK + 2 * MIX + GLA_LORA
RWKV_COLS = 3 * MIX + RWKV_W_LORA + RWKV_A_LORA + RWKV_G_LORA
RET_COLS = 2 * RET_QK + 2 * MIX
GATE_COLS = 3 * D_MODEL

LANES = 128
VMEM_LIMIT = 56 * 1024 * 1024

SEG = 3 * MIX
N_MAIN = GATE_COLS + 3 * SEG
SM_WL, SM_AL, SM_GL, SM_LR = 0, 128, 256, 512
N_SMALL = 640
FF_PAD = 5632

TM = 512
TM_IN = 512
TN_IN = 3072
TM_MERGE = 256
TM_FF = 1024
TN_FF = 512
ROWS = 512
PREPARE_LEAD = 2


def _cparams(n_axes):
    return pltpu.CompilerParams(dimension_semantics=("arbitrary",) * n_axes,
                                vmem_limit_bytes=VMEM_LIMIT)


def _dot(a, b):
    return jnp.dot(a.astype(BF16), b.astype(BF16), preferred_element_type=F32)


def _dot_nt(a, b):
    return lax.dot_general(a.astype(BF16), b.astype(BF16), (((1,), (1,)), ((), ())),
                           preferred_element_type=F32)


def _dot_tn(a, b):
    return lax.dot_general(a.astype(BF16), b.astype(BF16), (((0,), (0,)), ((), ())),
                           preferred_element_type=F32)


def _split(x, parts):
    out = []
    for _ in range(parts):
        p = x.astype(BF16)
        out.append(p)
        x = x - p.astype(F32)
    return out


def _dot_lhs_exact(m_bf16, x, parts):
    acc = None
    for p in _split(x, parts):
        t = jnp.dot(m_bf16, p, preferred_element_type=F32)
        acc = t if acc is None else acc + t
    return acc


def _sigmoid(x):
    return 1.0 / (1.0 + jnp.exp(-x))


def _silu(x):
    return x * _sigmoid(x)


def _softplus(x):
    return jnp.maximum(x, 0.0) + jnp.log(1.0 + jnp.exp(-jnp.abs(x)))


def _rms_rows(x, gain):
    return x * lax.rsqrt(jnp.mean(x * x, axis=-1, keepdims=True) + EPS) * gain


def _tri_incl(n):
    r = lax.broadcasted_iota(jnp.int32, (n, n), 0)
    c = lax.broadcasted_iota(jnp.int32, (n, n), 1)
    return r >= c


def _in_proj_kernel(x_ref, g_ref, w_ref, ws_ref, o_ref, os_ref, h_ref):
    @pl.when(pl.program_id(1) == 0)
    def _():
        h = _rms_rows(x_ref[...], g_ref[...]).astype(BF16)
        h_ref[...] = h
        os_ref[...] = jnp.dot(h, ws_ref[...], preferred_element_type=F32)

    o_ref[...] = jnp.dot(h_ref[...], w_ref[...], preferred_element_type=F32).astype(o_ref.dtype)


def _in_proj(x, gain, w_main, w_small):
    t, d = x.shape
    return pl.pallas_call(
        _in_proj_kernel,
        grid=(t // TM_IN, N_MAIN // TN_IN),
        in_specs=[pl.BlockSpec((TM_IN, d), lambda i, j: (i, 0)),
                  pl.BlockSpec((1, d), lambda i, j: (0, 0)),
                  pl.BlockSpec((d, TN_IN), lambda i, j: (0, j)),
                  pl.BlockSpec((d, N_SMALL), lambda i, j: (0, 0))],
        out_specs=[pl.BlockSpec((TM_IN, TN_IN), lambda i, j: (i, j)),
                   pl.BlockSpec((TM_IN, N_SMALL), lambda i, j: (i, 0))],
        out_shape=[jax.ShapeDtypeStruct((t, N_MAIN), BF16), jax.ShapeDtypeStruct((t, N_SMALL), F32)],
        scratch_shapes=[pltpu.VMEM((TM_IN, d), BF16)],
        compiler_params=_cparams(2),
        name="in_proj",
    )(x, gain.reshape(1, d), w_main, w_small)


def _rope_kernel(pos_ref, f_ref, s_ref, cos_ref, sin_ref):
    ang = pos_ref[...] * f_ref[...]
    cos_ref[...] = jnp.cos(ang)
    sin_ref[...] = jnp.sin(ang) * s_ref[...]


def _rope_tables(pos):
    t = pos.shape[0]
    half = RET_DK // 2
    inv = (ROPE_BASE ** (-np.arange(0, RET_DK, 2, dtype=np.float32) / np.float32(RET_DK))).astype(np.float32)
    inv2 = jnp.asarray(np.concatenate([inv, inv]).reshape(1, RET_DK))
    sign = jnp.asarray(np.concatenate([-np.ones(half, np.float32), np.ones(half, np.float32)]).reshape(1, RET_DK))
    rows = TM
    return pl.pallas_call(
        _rope_kernel,
        grid=(t // rows,),
        in_specs=[pl.BlockSpec((rows, 1), lambda i: (i, 0)),
                  pl.BlockSpec((1, RET_DK), lambda i: (0, 0)),
                  pl.BlockSpec((1, RET_DK), lambda i: (0, 0))],
        out_specs=[pl.BlockSpec((rows, RET_DK), lambda i: (i, 0))] * 2,
        out_shape=[jax.ShapeDtypeStruct((t, RET_DK), F32)] * 2,
        compiler_params=_cparams(1),
        name="rope_tables",
    )(pos, inv2, sign)


def _gla_kernel(u_ref, lr_ref, wdh_ref, wdl_ref, bd_ref, ng_ref, o_ref, st_ref, la_ref, *, steps_per_seq):
    @pl.when(pl.program_id(0) % steps_per_seq == 0)
    def _():
        st_ref[...] = jnp.zeros_like(st_ref)

    lr_hi, lr_lo = _split(lr_ref[...], 2)
    z = (jnp.dot(lr_hi, wdh_ref[...], preferred_element_type=F32)
         + (jnp.dot(lr_lo, wdh_ref[...], preferred_element_type=F32)
            + jnp.dot(lr_hi, wdl_ref[...], preferred_element_type=F32))) + bd_ref[...]
    la_ref[...] = -_softplus(-z) * (1.0 / GLA_GATE_NORMALIZER)

    causal = _tri_incl(CHUNK)
    tri = causal.astype(BF16)
    scale = GLA_DK ** -0.5
    ng = ng_ref[...]

    heads = range(GLA_HEADS)
    kcol = [slice(h * GLA_DK, (h + 1) * GLA_DK) for h in heads]
    vcol = [slice(h * GLA_DV, (h + 1) * GLA_DV) for h in heads]

    chunks = range(ROWS // CHUNK)
    units = [(c, h) for c in chunks for h in heads]
    rows = [slice(c * CHUNK, (c + 1) * CHUNK) for c in chunks]
    seg = lambda off, col: [u_ref[rows[c], off + col[h].start:off + col[h].stop] for c, h in units]
    q, k = seg(0, kcol), seg(GLA_QK, kcol)
    vb, g = seg(2 * GLA_QK, vcol), seg(2 * GLA_QK + MIX, vcol)
    cum_c = [_dot_lhs_exact(tri, la_ref[rows[c], :], 2) for c in chunks]
    cum = [cum_c[c][:, kcol[h]] for c, h in units]
    n = range(len(units))
    cl = [cum[i][CHUNK - 1:CHUNK, :] for i in n]
    q_in = [(q[i] * jnp.exp(cum[i]) * scale).astype(BF16) for i in n]
    k_in = [k[i] * jnp.exp(-cum[i]) for i in n]
    k_st = [k[i] * jnp.exp(cl[i] - cum[i]) for i in n]
    sc = [jnp.where(causal, _dot_nt(q_in[i], k_in[i]), 0.0) for i in n]
    kv = [_dot_tn(vb[i], k_st[i]) for i in n]
    o = [_dot(sc[i], vb[i]) for i in n]
    st = [None] * len(units)
    for h in heads:
        cur = st_ref[h]
        for c in chunks:
            i = c * GLA_HEADS + h
            st[i] = cur
            cur = cur * jnp.exp(cl[i]) + kv[i]
        st_ref[h] = cur
    o = [o[i] + _dot_nt(q_in[i], st[i]) for i in n]
    for i, (c, h) in enumerate(units):
        y = o[i] * lax.rsqrt(jnp.mean(o[i] * o[i], axis=-1, keepdims=True) + EPS) * ng
        o_ref[rows[c], vcol[h]] = (y * _silu(g[i].astype(F32))).astype(o_ref.dtype)


def _gla(u, small, wd_pad, b_decay, norm_g, seq):
    t = u.shape[0]
    wd_hi = wd_pad.astype(BF16)
    wd_hi_residual = (wd_pad - wd_hi.astype(F32)).astype(BF16)
    return pl.pallas_call(
        functools.partial(_gla_kernel, steps_per_seq=seq // ROWS),
        grid=(t // ROWS,),
        in_specs=[pl.BlockSpec((ROWS, SEG), lambda i: (i, GATE_COLS // SEG)),
                  pl.BlockSpec((ROWS, LANES), lambda i: (i, SM_LR // LANES)),
                  pl.BlockSpec((LANES, GLA_QK), lambda i: (0, 0)),
                  pl.BlockSpec((LANES, GLA_QK), lambda i: (0, 0)),
                  pl.BlockSpec((1, GLA_QK), lambda i: (0, 0)),
                  pl.BlockSpec((1, GLA_DV), lambda i: (0, 0))],
        out_specs=pl.BlockSpec((ROWS, MIX), lambda i: (i, 0)),
        out_shape=jax.ShapeDtypeStruct((t, MIX), BF16),
        scratch_shapes=[pltpu.VMEM((GLA_HEADS, GLA_DV, GLA_DK), F32),
                        pltpu.VMEM((ROWS, GLA_QK), F32)],
        compiler_params=_cparams(1),
        name="gla_mixer",
    )(u, small, wd_hi, wd_hi_residual, b_decay.reshape(1, GLA_QK), norm_g.reshape(1, GLA_DV))


def _ret_consts():
    lg = np.log(1.0 - np.exp2(-5.0 - np.arange(RET_HEADS, dtype=np.float32))).astype(np.float32)
    pos = np.arange(CHUNK, dtype=np.float32)
    diff = pos[:, None] - pos[None, :]
    causal = diff >= 0
    dmask = np.where(causal, np.exp(lg[:, None, None] * np.where(causal, diff, 0.0)), 0.0).astype(np.float32)
    qs = np.exp(lg[:, None] * (pos[None, :] + 1.0)).astype(np.float32)
    vs = np.exp(lg[:, None] * (CHUNK - 1.0 - pos[None, :])).astype(np.float32)
    cd = [float(np.exp(np.float32(l) * np.float32(CHUNK))) for l in lg]
    qs_b = np.broadcast_to(qs[:, :, None], (RET_HEADS, CHUNK, RET_DK)).copy()
    vs_b = np.broadcast_to(vs[:, :, None], (RET_HEADS, CHUNK, RET_DK)).copy()
    return dmask, qs_b, vs_b, cd


def _ret_kernel(u_ref, cos_ref, sin_ref, dm_ref, qs_ref, vs_ref, o_ref, st_ref, *, steps_per_seq, chunk_decay):
    @pl.when(pl.program_id(0) % steps_per_seq == 0)
    def _():
        st_ref[...] = jnp.zeros_like(st_ref)

    scale = RET_DK ** -0.5
    half = RET_DK // 2

    heads = range(RET_HEADS)
    kcol = [slice(h * RET_DK, (h + 1) * RET_DK) for h in heads]
    vcol = [slice(h * RET_DV, (h + 1) * RET_DV) for h in heads]

    chunks = range(ROWS // CHUNK)
    units = [(c, h) for c in chunks for h in heads]
    rows = [slice(c * CHUNK, (c + 1) * CHUNK) for c in chunks]
    cos = [cos_ref[rows[c], :] for c in chunks]
    sin = [sin_ref[rows[c], :] for c in chunks]
    seg = lambda off, col: [u_ref[rows[c], off + col[h].start:off + col[h].stop] for c, h in units]
    q, k = seg(0, kcol), seg(RET_QK, kcol)
    vb, g = seg(2 * RET_QK, vcol), seg(2 * RET_QK + MIX, vcol)
    n = range(len(units))
    rot = lambda x, c: x * cos[c] + pltpu.roll(x, half, 1) * sin[c]
    q = [rot(q[i].astype(F32), c) * scale for i, (c, h) in enumerate(units)]
    k = [rot(k[i].astype(F32), c) for i, (c, h) in enumerate(units)]
    sc = [_dot_nt(q[i], k[i]) * dm_ref[h] for i, (c, h) in enumerate(units)]
    kv = [_dot_tn(vb[i], k[i] * vs_ref[h]) for i, (c, h) in enumerate(units)]
    o = [_dot(sc[i], vb[i]) for i in n]
    st = [None] * len(units)
    for h in heads:
        cur = st_ref[h]
        for c in chunks:
            i = c * RET_HEADS + h
            st[i] = cur
            cur = cur * chunk_decay[h] + kv[i]
        st_ref[h] = cur
    o = [o[i] + _dot_nt(q[i] * qs_ref[h], st[i]) for i, (c, h) in enumerate(units)]
    for i, (c, h) in enumerate(units):
        y = o[i] * lax.rsqrt(jnp.mean(o[i] * o[i], axis=-1, keepdims=True) + EPS)
        o_ref[rows[c], vcol[h]] = (_silu(g[i].astype(F32)) * y).astype(o_ref.dtype)


def _retention(u, cos2, sin2, seq):
    t = u.shape[0]
    dmask, qs_b, vs_b, cd = _ret_consts()
    return pl.pallas_call(
        functools.partial(_ret_kernel, steps_per_seq=seq // ROWS, chunk_decay=tuple(cd)),
        grid=(t // ROWS,),
        in_specs=[pl.BlockSpec((ROWS, SEG), lambda i: (i, GATE_COLS // SEG + 2)),
                  pl.BlockSpec((ROWS, RET_DK), lambda i: (i, 0)),
                  pl.BlockSpec((ROWS, RET_DK), lambda i: (i, 0)),
                  pl.BlockSpec((RET_HEADS, CHUNK, CHUNK), lambda i: (0, 0, 0)),
                  pl.BlockSpec((RET_HEADS, CHUNK, RET_DK), lambda i: (0, 0, 0)),
                  pl.BlockSpec((RET_HEADS, CHUNK, RET_DK), lambda i: (0, 0, 0))],
        out_specs=pl.BlockSpec((ROWS, MIX), lambda i: (i, 0)),
        out_shape=jax.ShapeDtypeStruct((t, MIX), BF16),
        scratch_shapes=[pltpu.VMEM((RET_HEADS, RET_DV, RET_DK), F32)],
        compiler_params=_cparams(1),
        name="retention_mixer",
    )(u, cos2, sin2, jnp.asarray(dmask), jnp.asarray(qs_b), jnp.asarray(vs_b))


def _rwkv_kernel(*refs, steps_per_seq, first_layer):
    if first_layer:
        (u_ref, sm_ref, mu_ref, mus_ref, w0_ref, w2_ref, a0_ref, a2_ref, g2_ref, kk_ref, ka_ref,
         rk_ref, lnw_ref, lnb_ref, o_ref, vf_out_ref,
         st_ref, cu_ref, cs_ref, r_s, lw_s, k_s, v_s, kk_s, a_s, g_s) = refs
    else:
        (u_ref, sm_ref, mu_ref, mus_ref, w0_ref, w2_ref, a0_ref, a2_ref, g2_ref, kk_ref, ka_ref,
         rk_ref, lnw_ref, lnb_ref, vf_ref, v0_ref, v1_ref, v2_ref, o_ref,
         st_ref, cu_ref, cs_ref, r_s, lw_s, k_s, v_s, kk_s, a_s, g_s) = refs

    @pl.when(pl.program_id(0) % steps_per_seq == 0)
    def _():
        st_ref[...] = jnp.zeros_like(st_ref)
        cu_ref[...] = jnp.zeros_like(cu_ref)
        cs_ref[...] = jnp.zeros_like(cs_ref)

    row = lax.broadcasted_iota(jnp.int32, (ROWS, 1), 0)

    def shifted(x, carry_ref, mu):
        prev = jnp.where(row == 0, carry_ref[0:1, :], pltpu.roll(x, 1, 0))
        carry_ref[0:1, :] = x[ROWS - 1:ROWS, :]
        return x + (prev - x) * mu

    us = shifted(u_ref[...].astype(F32), cu_ref, mu_ref[...])
    ss = shifted(sm_ref[...], cs_ref, mus_ref[...])
    r = us[:, 0:MIX]
    k = us[:, MIX:2 * MIX]
    v = us[:, 2 * MIX:3 * MIX]
    wl = ss[:, SM_WL:SM_WL + LANES]
    al = ss[:, SM_AL:SM_AL + LANES]
    gl = ss[:, SM_GL:SM_GL + 2 * LANES]
    w = w0_ref[...] + _dot(jnp.tanh(wl), w2_ref[...])
    lw_s[...] = -math.exp(-0.5) * _sigmoid(w)
    a = _sigmoid(a0_ref[...] + _dot(al, a2_ref[...]))
    g_s[...] = _dot(_sigmoid(gl), g2_ref[...])
    if first_layer:
        vf_out_ref[...] = v
    else:
        mix = _sigmoid(v0_ref[...] + _dot(_dot(v, v1_ref[...]), v2_ref[...]))
        v = v + (vf_ref[...] - v) * mix
    r_s[...] = r
    v_s[...] = v
    a_s[...] = a
    kk_s[...] = k * kk_ref[...]
    k_s[...] = k * (1.0 + (a - 1.0) * ka_ref[...])

    lane = lax.broadcasted_iota(jnp.int32, (1, LANES), 1)
    mlo = (lane < RWKV_HEAD).astype(F32)
    mhi = 1.0 - mlo
    rr = lax.broadcasted_iota(jnp.int32, (LANES, LANES), 0)
    cc = lax.broadcasted_iota(jnp.int32, (LANES, LANES), 1)
    bd_mask = (rr // RWKV_HEAD) == (cc // RWKV_HEAD)
    bd = bd_mask.astype(BF16)
    eye = (rr == cc).astype(F32)
    tri = _tri_incl(CHUNK).astype(BF16)
    trow = lax.broadcasted_iota(jnp.int32, (CHUNK, LANES), 0)
    scol = lax.broadcasted_iota(jnp.int32, (CHUNK, LANES), 1) % RWKV_HEAD
    strict = scol < trow
    incl = scol <= trow
    inv_n = 1.0 / RWKV_HEAD

    def stack(x):
        return jnp.concatenate([x * mlo, x * mhi], axis=0)

    pairs = range(RWKV_PAIRS)
    cols = [slice(p * LANES, (p + 1) * LANES) for p in pairs]
    n_double = int(math.log2(CHUNK)) - 1

    def head_sums(tiles):
        s_all = jnp.dot(jnp.concatenate(tiles, axis=0).astype(BF16), bd, preferred_element_type=F32)
        return [s_all[p * CHUNK:(p + 1) * CHUNK] for p in range(len(tiles))]

    def prepare(c, ctx):
        rows = slice(c * CHUNK, (c + 1) * CHUNK)
        tile = lambda ref: [ref[rows, cols[p]] for p in pairs]
        r_t, lw_t, k_t, v_t, kkr, a_t = map(tile, (r_s, lw_s, k_s, v_s, kk_s, a_s))
        nrm = head_sums([kkr[p] * kkr[p] for p in pairs])
        cum_all = _dot_lhs_exact(tri, lw_s[rows, :], 2)
        yield
        cum = [cum_all[:, cols[p]] for p in pairs]
        kk = [kkr[p] * lax.rsqrt(jnp.maximum(nrm[p], 1e-24)) for p in pairs]
        bv = [kk[p] * a_t[p] for p in pairs]
        cl = [cum[p][CHUNK - 1:CHUNK, :] for p in pairs]
        e_out = [jnp.exp(-cum[p]) for p in pairs]
        e_end = [e_out[p] * jnp.exp(cl[p]) for p in pairs]
        at = [-kk[p] * jnp.exp(cum[p] - lw_t[p]) for p in pairs]
        rt = [r_t[p] * jnp.exp(cum[p]) for p in pairs]
        gm = [_dot_nt(jnp.concatenate([at[p], rt[p]], axis=0),
                      jnp.concatenate([stack(bv[p] * e_out[p]), stack(k_t[p] * e_out[p])], axis=0))
              for p in pairs]
        yield
        ab = [jnp.where(strict, gm[p][0:CHUNK, 0:LANES], 0.0) for p in pairs]
        ak = [jnp.where(strict, gm[p][0:CHUNK, LANES:2 * LANES], 0.0) for p in pairs]
        rb = [jnp.where(incl, gm[p][CHUNK:2 * CHUNK, 0:LANES], 0.0) for p in pairs]
        rk = [jnp.where(incl, gm[p][CHUNK:2 * CHUNK, LANES:2 * LANES], 0.0) for p in pairs]
        vs2 = [stack(v_t[p]) for p in pairs]
        npow = [stack(ab[p]) for p in pairs]
        tinv = [eye + npow[p] for p in pairs]
        npow = [_dot(npow[p], npow[p]) for p in pairs]
        akv = [_dot(ak[p], vs2[p]) for p in pairs]
        rkv = [_dot(rk[p], vs2[p]) for p in pairs]
        yield
        for _ in range(n_double - 1):
            prod = [_dot(tinv[p], npow[p]) for p in pairs]
            npow = [_dot(npow[p], npow[p]) for p in pairs]
            yield
            tinv = [tinv[p] + prod[p] for p in pairs]
        prod = [_dot(tinv[p], npow[p]) for p in pairs]
        rkk = head_sums([r_t[p] * k_t[p] * rk_ref[:, cols[p]] for p in pairs])
        yield
        tinv = [tinv[p] + prod[p] for p in pairs]
        ctx.update(
            tcat=[tinv[p][0:CHUNK] + tinv[p][CHUNK:2 * CHUNK] for p in pairs],
            kb_end=[jnp.concatenate([bv[p] * e_end[p], k_t[p] * e_end[p]], axis=0) for p in pairs],
            bonus=[rkk[p] * v_t[p] for p in pairs],
            at=at, rt=rt, akv=akv, rkv=rkv, rb=rb, cl=cl, v_t=v_t)

    def apply(c, ctx):
        rows = slice(c * CHUNK, (c + 1) * CHUNK)
        st = [st_ref[p] for p in pairs]
        u_rhs = [_dot_nt(ctx["at"][p], st[p]) + ctx["akv"][p] for p in pairs]
        y_st = [_dot_nt(ctx["rt"][p], st[p]) for p in pairs]
        yield
        u_t = [_dot(ctx["tcat"][p], stack(u_rhs[p])) for p in pairs]
        yield
        y = [y_st[p] + _dot(ctx["rb"][p], stack(u_t[p])) + ctx["rkv"][p] for p in pairs]
        upd = [_dot_tn(jnp.concatenate([u_t[p], ctx["v_t"][p]], axis=0), ctx["kb_end"][p]) for p in pairs]
        yield
        for p in pairs:
            st_ref[p] = st[p] * jnp.exp(ctx["cl"][p]) + jnp.where(bd_mask, upd[p], 0.0)
        mean = head_sums(y)
        yield
        dev = [y[p] - mean[p] * inv_n for p in pairs]
        var = head_sums([dev[p] * dev[p] for p in pairs])
        yield
        for p in pairs:
            yn = dev[p] * lax.rsqrt(var[p] * inv_n + RWKV_GN_EPS) * lnw_ref[:, cols[p]] + lnb_ref[:, cols[p]]
            o_ref[rows, cols[p]] = ((yn + ctx["bonus"][p]) * g_s[rows, cols[p]]).astype(o_ref.dtype)

    def run_interleaved(*gens):
        live = list(gens)
        while live:
            for g in list(live):
                if next(g, StopIteration) is StopIteration:
                    live.remove(g)

    n_chunks = ROWS // CHUNK
    ctx = {}
    run_interleaved(prepare(0, ctx))
    for c in range(n_chunks):
        nxt = {}
        if c + 1 < n_chunks:
            ahead = prepare(c + 1, nxt)
            for _ in range(PREPARE_LEAD):
                next(ahead)
            run_interleaved(apply(c, ctx), ahead)
        else:
            run_interleaved(apply(c, ctx))
        ctx = nxt


def _rwkv(u, small, prm, v_first, seq):
    t = u.shape[0]
    first = v_first is None
    row = lambda a: a.reshape(1, -1)
    vec = lambda n: pl.BlockSpec((1, n), lambda i: (0, 0))
    mat = lambda r, c: pl.BlockSpec((r, c), lambda i: (0, 0))
    args = [u, small, row(prm["mu"]), row(prm["mu_s"]), row(prm["w0"]), prm["w2"], row(prm["a0"]), prm["a2"],
            prm["g2"], row(prm["k_k"]), row(prm["k_a"]), row(prm["r_k"]), row(prm["ln_w"]), row(prm["ln_b"])]
    in_specs = [pl.BlockSpec((ROWS, SEG), lambda i: (i, GATE_COLS // SEG + 1)),
                pl.BlockSpec((ROWS, 4 * LANES), lambda i: (i, 0)),
                vec(SEG), vec(4 * LANES), vec(MIX), mat(LANES, MIX), vec(MIX), mat(LANES, MIX),
                mat(2 * LANES, MIX), vec(MIX), vec(MIX), vec(MIX), vec(MIX), vec(MIX)]
    out_block = pl.BlockSpec((ROWS, MIX), lambda i: (i, 0))
    if first:
        out_specs = [out_block, out_block]
        out_shape = [jax.ShapeDtypeStruct((t, MIX), BF16), jax.ShapeDtypeStruct((t, MIX), F32)]
    else:
        args += [v_first, row(prm["v0"]), prm["v1"], prm["v2"]]
        in_specs += [out_block, vec(MIX), mat(MIX, LANES), mat(LANES, MIX)]
        out_specs = out_block
        out_shape = jax.ShapeDtypeStruct((t, MIX), BF16)
    tok = pltpu.VMEM((ROWS, MIX), F32)
    res = pl.pallas_call(
        functools.partial(_rwkv_kernel, steps_per_seq=seq // ROWS, first_layer=first),
        grid=(t // ROWS,),
        in_specs=in_specs,
        out_specs=out_specs,
        out_shape=out_shape,
        scratch_shapes=[pltpu.VMEM((RWKV_PAIRS, LANES, LANES), F32),
                        pltpu.VMEM((8, SEG), F32), pltpu.VMEM((8, 4 * LANES), F32),
                        tok, tok, tok, tok, tok, tok, tok],
        compiler_params=_cparams(1),
        name="rwkv7_mixer",
    )(*args)
    return (res[0], res[1]) if first else (res, v_first)


def _merge_kernel(x_ref, oa_ref, ob_ref, oc_ref, g_ref, wa_ref, wb_ref, wc_ref, wo_ref, o_ref):
    d = D_MODEL
    gate = lambda k: _sigmoid(g_ref[:, k * d:(k + 1) * d].astype(F32))
    m = (gate(0) * jnp.dot(oa_ref[...], wa_ref[...], preferred_element_type=F32)
         + gate(1) * jnp.dot(ob_ref[...], wb_ref[...], preferred_element_type=F32)
         + gate(2) * jnp.dot(oc_ref[...], wc_ref[...], preferred_element_type=F32))
    o_ref[...] = x_ref[...] + jnp.dot(m.astype(BF16), wo_ref[...], preferred_element_type=F32)


def _merge_out(x, u, o_a, o_b, o_c, wa, wb, wc, wo):
    t, d = x.shape
    full = pl.BlockSpec((TM_MERGE, d), lambda i: (i, 0))
    mixo = pl.BlockSpec((TM_MERGE, MIX), lambda i: (i, 0))
    resident = lambda r, c: pl.BlockSpec((r, c), lambda i: (0, 0), pipeline_mode=pl.Buffered(1))
    return pl.pallas_call(
        _merge_kernel,
        grid=(t // TM_MERGE,),
        in_specs=[full, mixo, mixo, mixo, pl.BlockSpec((TM_MERGE, GATE_COLS), lambda i: (i, 0)),
                  resident(MIX, d), resident(MIX, d), resident(MIX, d), resident(d, d)],
        out_specs=full,
        out_shape=jax.ShapeDtypeStruct((t, d), F32),
        compiler_params=_cparams(1),
        name="merge_out_proj",
    )(x, o_a, o_b, o_c, u, wa, wb, wc, wo)


def _ffn_kernel(x_ref, g_ref, wz_ref, wv_ref, cw_ref, cb_ref, wd_ref, o_ref, h_ref, carry_ref, *,
                tiles_per_seq):
    i, j = pl.program_id(0), pl.program_id(1)

    @pl.when(j == 0)
    def _():
        x = x_ref[...]
        h_ref[...] = _rms_rows(x, g_ref[...]).astype(BF16)
        o_ref[...] = x

    @pl.when(jnp.logical_and(j == 0, i % tiles_per_seq == 0))
    def _():
        carry_ref[...] = jnp.zeros_like(carry_ref)

    h = h_ref[...]
    z = jnp.dot(h, wz_ref[...], preferred_element_type=F32)
    val = jnp.dot(h, wv_ref[...], preferred_element_type=F32)
    slot = pl.ds(pl.multiple_of(j * 8, 8), 8)
    tail = carry_ref[slot, :]
    carry_ref[slot, :] = z[TM_FF - 8:TM_FF, :]
    row = lax.broadcasted_iota(jnp.int32, (TM_FF, 1), 0)
    z1 = jnp.where(row == 0, tail[7:8, :], pltpu.roll(z, 1, 0))
    z2 = jnp.where(row == 0, tail[6:7, :], jnp.where(row == 1, tail[7:8, :], pltpu.roll(z, 2, 0)))
    cw = cw_ref[...]
    zc = z * cw[2:3, :] + z1 * cw[1:2, :] + z2 * cw[0:1, :] + cb_ref[...]
    gelu = 0.5 * zc * (1.0 + jnp.tanh(math.sqrt(2.0 / math.pi) * (zc + GELU_CUBIC * (zc * zc * zc))))
    o_ref[...] += jnp.dot((gelu * val).astype(BF16), wd_ref[...], preferred_element_type=F32)


def _ffn(x, gain, wz, wv, cw, cb, wd, seq):
    t, d = x.shape
    nj = FF_PAD // TN_FF
    full = pl.BlockSpec((TM_FF, d), lambda i, j: (i, 0))
    return pl.pallas_call(
        functools.partial(_ffn_kernel, tiles_per_seq=seq // TM_FF),
        grid=(t // TM_FF, nj),
        in_specs=[full, pl.BlockSpec((1, d), lambda i, j: (0, 0)),
                  pl.BlockSpec((d, TN_FF), lambda i, j: (0, j)),
                  pl.BlockSpec((d, TN_FF), lambda i, j: (0, j)),
                  pl.BlockSpec((CONV_W, TN_FF), lambda i, j: (0, j)),
                  pl.BlockSpec((1, TN_FF), lambda i, j: (0, j)),
                  pl.BlockSpec((TN_FF, d), lambda i, j: (j, 0))],
        out_specs=full,
        out_shape=jax.ShapeDtypeStruct((t, d), F32),
        scratch_shapes=[pltpu.VMEM((TM_FF, d), BF16), pltpu.VMEM((nj * 8, TN_FF), F32)],
        compiler_params=_cparams(2),
        name="ffn_geglu_conv",
    )(x, gain.reshape(1, d), wz, wv, cw, cb.reshape(1, FF_PAD), wd)


def _ple_kernel(x_ref, g_ref, wg_ref, p_ref, wp_ref, fg_ref, o_ref, *, final):
    x = x_ref[...]
    h = _rms_rows(x, g_ref[...])
    gate = _sigmoid(_dot(h, wg_ref[...]))
    y = x + gate * _dot(p_ref[...], wp_ref[...])
    if final:
        y = _rms_rows(y, fg_ref[...])
    o_ref[...] = y


def _ple(x, gain, wg, p, wp, final_gain, final):
    t, d = x.shape
    full = pl.BlockSpec((TM, d), lambda i: (i, 0))
    vec = pl.BlockSpec((1, d), lambda i: (0, 0))
    return pl.pallas_call(
        functools.partial(_ple_kernel, final=final),
        grid=(t // TM,),
        in_specs=[full, vec, pl.BlockSpec((d, d), lambda i: (0, 0)),
                  pl.BlockSpec((TM, PLE_DIM), lambda i: (i, 0)),
                  pl.BlockSpec((PLE_DIM, d), lambda i: (0, 0)), vec],
        out_specs=full,
        out_shape=jax.ShapeDtypeStruct((t, d), F32),
        compiler_params=_cparams(1),
        name="ple_embed",
    )(x, gain.reshape(1, d), wg, p, wp, final_gain.reshape(1, d))


def _pad_cols(a, n):
    return jnp.pad(a, ((0, 0), (0, n - a.shape[1])))


def _pad_rows(a, n):
    return jnp.pad(a, ((0, n - a.shape[0]), (0, 0)))


def _pad_vec(a, n):
    return jnp.pad(a, (0, n - a.shape[0]))


def _layout_w_in(w):
    o_rwkv = GLA_COLS
    o_ret = GLA_COLS + RWKV_COLS
    o_gate = o_ret + RET_COLS
    main = jnp.concatenate([w[:, o_gate:o_gate + GATE_COLS], w[:, 0:SEG],
                            w[:, o_rwkv:o_rwkv + SEG], w[:, o_ret:o_ret + SEG]], axis=1)
    o = o_rwkv + SEG
    small = jnp.concatenate([
        _pad_cols(w[:, o:o + RWKV_W_LORA], LANES),
        _pad_cols(w[:, o + RWKV_W_LORA:o + RWKV_W_LORA + RWKV_A_LORA], LANES),
        _pad_cols(w[:, o + RWKV_W_LORA + RWKV_A_LORA:o + RWKV_W_LORA + RWKV_A_LORA + RWKV_G_LORA], 2 * LANES),
        _pad_cols(w[:, SEG:SEG + GLA_LORA], LANES)], axis=1)
    return main.astype(BF16), small.astype(BF16)


def _layout_mu(mu):
    o = SEG
    small = jnp.concatenate([
        _pad_vec(mu[o:o + RWKV_W_LORA], LANES),
        _pad_vec(mu[o + RWKV_W_LORA:o + RWKV_W_LORA + RWKV_A_LORA], LANES),
        _pad_vec(mu[o + RWKV_W_LORA + RWKV_A_LORA:], 2 * LANES)])
    return mu[:SEG], small


def kernel(x, p, positions, mix_norm, w_in, gla_w_decay, gla_b_decay, gla_norm, rwkv_mu, rwkv_w0, rwkv_w2, rwkv_a0, rwkv_a2, rwkv_g2, rwkv_k_k, rwkv_k_a, rwkv_r_k, rwkv_ln_w, rwkv_ln_b, rwkv_v0, rwkv_v1, rwkv_v2, w_br_gla, w_br_rwkv, w_br_ret, w_o, ffn_norm, w_up, conv_w, conv_b, w_down, ple_norm, w_ple_gate, w_ple_proj, final_norm):
    b, s, d = x.shape
    depth = w_in.shape[0]
    t = b * s
    assert d == D_MODEL and s % TM_FF == 0 and s % ROWS == 0 and t % TM_IN == 0
    xf = x.reshape(t, d)
    cos2, sin2 = _rope_tables(positions.astype(F32).reshape(t, 1))
    v_first = None
    for i in range(depth):
        w_main, w_small = _layout_w_in(w_in[i])
        u, small = _in_proj(xf, mix_norm[i], w_main, w_small)
        o_a = _gla(u, small, _pad_rows(gla_w_decay[i], LANES), gla_b_decay[i], gla_norm[i], s)
        mu_main, mu_small = _layout_mu(rwkv_mu[i])
        prm = dict(mu=mu_main, mu_s=mu_small, w0=rwkv_w0[i], w2=_pad_rows(rwkv_w2[i], LANES).astype(BF16),
                   a0=rwkv_a0[i], a2=_pad_rows(rwkv_a2[i], LANES).astype(BF16),
                   g2=_pad_rows(rwkv_g2[i], 2 * LANES).astype(BF16), k_k=rwkv_k_k[i], k_a=rwkv_k_a[i],
                   r_k=rwkv_r_k[i].reshape(MIX), ln_w=rwkv_ln_w[i], ln_b=rwkv_ln_b[i])
        if i > 0:
            prm.update(v0=rwkv_v0[i - 1], v1=_pad_cols(rwkv_v1[i - 1], LANES).astype(BF16),
                       v2=_pad_rows(rwkv_v2[i - 1], LANES).astype(BF16))
        o_b, v_first = _rwkv(u, small, prm, v_first, s)
        o_c = _retention(u, cos2, sin2, s)
        xf = _merge_out(xf, u, o_a, o_b, o_c, w_br_gla[i].astype(BF16), w_br_rwkv[i].astype(BF16),
                        w_br_ret[i].astype(BF16), w_o[i].astype(BF16))
        wz = _pad_cols(w_up[i][:, :D_FF], FF_PAD).astype(BF16)
        wv = _pad_cols(w_up[i][:, D_FF:], FF_PAD).astype(BF16)
        xf = _ffn(xf, ffn_norm[i], wz, wv, _pad_cols(conv_w[i], FF_PAD), _pad_vec(conv_b[i], FF_PAD),
                  _pad_rows(w_down[i], FF_PAD).astype(BF16), s)
        xf = _ple(xf, ple_norm[i], w_ple_gate[i].astype(BF16), p[i].reshape(t, PLE_DIM),
                  w_ple_proj[i].astype(BF16), final_norm, final=(i == depth - 1))
    return xf.reshape(b, s, d)
```
